```python
import math
import jax, jax.numpy as jnp
from jax import lax
import numpy as np

D_MODEL = 1024
BATCH = 8
SEQ = 2048
DEPTH = 2

F32 = jnp.float32
EPS = 1e-6
N_SUB = 3
D_FF = ((8 * D_MODEL // 3 + 127) // 128) * 128
FFN_RES_WEIGHT = 0.5
D_MIX = D_MODEL
POOL_WINDOWS = (2, 4, 8, 16)
N_POOL = len(POOL_WINDOWS)
POOL_WIDTH = D_MIX // 2
POOL_GD = POOL_WIDTH // N_POOL
SGU_WIDTH = D_MIX // 2
SGU_HEADS = 4
SGU_HD = SGU_WIDTH // SGU_HEADS
CHUNK = 128
SSM_WIDTH = D_MIX
SSM_GROUP = 16
SSM_GROUPS = SSM_WIDTH // SSM_GROUP
SSM_STATE = 64
DT_MIN = 1e-3
DT_MAX = 1e-1
N_EVEN = (DEPTH + 1) // 2
N_ODD = DEPTH // 2

kernel_name = 'hybrid_pool_sgu_s5_macaron_adaln'


def rmsnorm(x, g):
    xf = x.astype(F32)
    y = xf * lax.rsqrt(jnp.mean(xf * xf, axis=-1, keepdims=True) + EPS)
    return (y * g.astype(F32)).astype(x.dtype)


def sublayer(x, fn, mod_k, g_pre, g_post, res_weight):
    shift, scale, gate = mod_k[:, 0, None, :], mod_k[:, 1, None, :], mod_k[:, 2, None, :]
    h = rmsnorm(x, g_pre) * (1.0 + scale) + shift
    y = rmsnorm(fn(h), g_post)
    return x + res_weight * gate * y


def swiglu(h, w_in, w_out):
    a, b = jnp.split(h @ w_in, 2, axis=-1)
    return (jax.nn.silu(a) * b) @ w_out


def pool_mixer(a, w_group, ch_scale):
    s_len = a.shape[1]
    cs = jnp.cumsum(a.astype(F32), axis=1)
    pos = jnp.arange(1, s_len + 1, dtype=F32)[None, :, None]
    diffs = []
    for g, w in enumerate(POOL_WINDOWS):
        sl = slice(g * POOL_GD, (g + 1) * POOL_GD)
        c_g = cs[..., sl]
        lagged = jnp.pad(c_g, ((0, 0), (w, 0), (0, 0)))[:, :s_len]
        mean = (c_g - lagged) / jnp.minimum(pos, float(w))
        diffs.append(mean - a[..., sl].astype(F32))
    d = jnp.stack(diffs, axis=2).astype(a.dtype)
    y = jnp.einsum('bsgi,gio->bsgo', d, w_group)
    return y.reshape(a.shape) * ch_scale


def sgu_mixer(z, ln_g, ln_b, w_s, b_s):
    u, v = jnp.split(z, 2, axis=-1)
    bsz, s_len, _ = u.shape
    vf = v.astype(F32).reshape(bsz, s_len, SGU_HEADS, SGU_HD)
    mu = jnp.mean(vf, axis=-1, keepdims=True)
    var = jnp.mean(jnp.square(vf - mu), axis=-1, keepdims=True)
    vn = ((vf - mu) * lax.rsqrt(var + EPS)).reshape(bsz, s_len, SGU_WIDTH) * ln_g + ln_b
    vn = vn.astype(u.dtype).reshape(bsz, s_len // CHUNK, CHUNK, SGU_HEADS, SGU_HD)
    causal = jnp.tril(jnp.ones((CHUNK, CHUNK), dtype=bool))
    w = jnp.where(causal[None], w_s, 0.0)
    s = jnp.einsum('hts,bcshd->bcthd', w, vn) + b_s.T[None, None, :, :, None]
    return u * s.reshape(bsz, s_len, SGU_WIDTH)


def s5_mixer(u, lam_re, lam_im, b_re, b_im, c_re, c_im, d_skip, log_dt, w_glu):
    bsz, s_len, _ = u.shape
    lam = lax.complex(lam_re.astype(F32), lam_im.astype(F32))
    dt = jnp.exp(log_dt.astype(F32))[:, None]
    lam_bar = jnp.exp(lam * dt)
    bmat = lax.complex(b_re.astype(F32), b_im.astype(F32))
    b_bar = ((lam_bar - 1.0) / lam)[..., None] * bmat
    cmat = lax.complex(c_re.astype(F32), c_im.astype(F32))
    uf = u.astype(F32)
    ug = uf.reshape(bsz, s_len, SSM_GROUPS, SSM_GROUP).astype(jnp.complex64)
    bu = jnp.einsum('gpn,bsgn->bsgp', b_bar, ug)
    a_seq = jnp.broadcast_to(lam_bar, (1, s_len) + lam_bar.shape)

    def combine(left, right):
        a_l, x_l = left
        a_r, x_r = right
        return a_r * a_l, a_r * x_l + x_r

    _, states = lax.associative_scan(combine, (a_seq, bu), axis=1)
    y = jnp.einsum('gnp,bsgp->bsgn', cmat, states).real.reshape(bsz, s_len, SSM_WIDTH)
    y = y + d_skip.astype(F32) * uf
    g = jax.nn.gelu(y).astype(u.dtype)
    a, b = jnp.split(g @ w_glu, 2, axis=-1)
    return a * jax.nn.sigmoid(b)


def setup_inputs(seed: int = 0) -> dict:
    key = jax.random.key(seed)
    ks = iter(jax.random.split(key, 32))

    def nrm(shape, scale):
        return jax.random.normal(next(ks), shape, F32) * scale

    D = D_MODEL
    x = nrm((BATCH, SEQ, D), 1.0)
    c = nrm((BATCH, D), 1.0)
    ada_w = nrm((DEPTH, D, N_SUB * 3 * D), 0.5 * D ** -0.5)
    ada_b = nrm((DEPTH, N_SUB * 3 * D), 0.02)
    norm_pre = 1.0 + nrm((DEPTH, N_SUB, D), 0.02)
    norm_post = 1.0 + nrm((DEPTH, N_SUB, D), 0.02)
    ffn_w_in = nrm((DEPTH, 2, D, 2 * D_FF), D ** -0.5)
    ffn_w_out = nrm((DEPTH, 2, D_FF, D), D_FF ** -0.5)
    ab_w_in = nrm((N_EVEN, D, POOL_WIDTH + 2 * SGU_WIDTH), D ** -0.5)
    pool_w = nrm((N_EVEN, N_POOL, POOL_GD, POOL_GD), POOL_GD ** -0.5)
    pool_scale = 1.0 + nrm((N_EVEN, POOL_WIDTH), 0.1)
    sgu_ln_g = 1.0 + nrm((N_EVEN, SGU_WIDTH), 0.02)
    sgu_ln_b = nrm((N_EVEN, SGU_WIDTH), 0.02)
    sgu_w = nrm((N_EVEN, SGU_HEADS, CHUNK, CHUNK), 0.5 * CHUNK ** -0.5)
    sgu_b = 1.0 + nrm((N_EVEN, SGU_HEADS, CHUNK), 0.02)
    ab_w_out = nrm((N_EVEN, POOL_WIDTH + SGU_WIDTH, D), (POOL_WIDTH + SGU_WIDTH) ** -0.5)
    ssm_w_in = nrm((N_ODD, D, SSM_WIDTH), D ** -0.5)
    n_idx = jnp.arange(SSM_STATE, dtype=F32)
    ssm_lam_re = -0.5 + nrm((N_ODD, SSM_GROUPS, SSM_STATE), 0.01)
    ssm_lam_im = jnp.pi * n_idx + nrm((N_ODD, SSM_GROUPS, SSM_STATE), 0.01)
    ssm_b_re = nrm((N_ODD, SSM_GROUPS, SSM_STATE, SSM_GROUP), (2 * SSM_GROUP) ** -0.5)
    ssm_b_im = nrm((N_ODD, SSM_GROUPS, SSM_STATE, SSM_GROUP), (2 * SSM_GROUP) ** -0.5)
    ssm_c_re = nrm((N_ODD, SSM_GROUPS, SSM_GROUP, SSM_STATE), SSM_STATE ** -0.5)
    ssm_c_im = nrm((N_ODD, SSM_GROUPS, SSM_GROUP, SSM_STATE), SSM_STATE ** -0.5)
    ssm_d = nrm((N_ODD, SSM_WIDTH), 1.0)
    ssm_log_dt = jax.random.uniform(next(ks), (N_ODD, SSM_GROUPS), F32,
                                    math.log(DT_MIN), math.log(DT_MAX))
    ssm_w_glu = nrm((N_ODD, SSM_WIDTH, 2 * D), SSM_WIDTH ** -0.5)
    return {'x': x, 'c': c, 'ada_w': ada_w, 'ada_b': ada_b,
            'norm_pre': norm_pre, 'norm_post': norm_post,
            'ffn_w_in': ffn_w_in, 'ffn_w_out': ffn_w_out,
            'ab_w_in': ab_w_in, 'pool_w': pool_w, 'pool_scale': pool_scale,
            'sgu_ln_g': sgu_ln_g, 'sgu_ln_b': sgu_ln_b, 'sgu_w': sgu_w, 'sgu_b': sgu_b,
            'ab_w_out': ab_w_out, 'ssm_w_in': ssm_w_in,
            'ssm_lam_re': ssm_lam_re, 'ssm_lam_im': ssm_lam_im,
            'ssm_b_re': ssm_b_re, 'ssm_b_im': ssm_b_im,
            'ssm_c_re': ssm_c_re, 'ssm_c_im': ssm_c_im,
            'ssm_d': ssm_d, 'ssm_log_dt': ssm_log_dt, 'ssm_w_glu': ssm_w_glu}


def reference(x, c, ada_w, ada_b, norm_pre, norm_post, ffn_w_in, ffn_w_out,
              ab_w_in, pool_w, pool_scale, sgu_ln_g, sgu_ln_b, sgu_w, sgu_b, ab_w_out,
              ssm_w_in, ssm_lam_re, ssm_lam_im, ssm_b_re, ssm_b_im, ssm_c_re, ssm_c_im,
              ssm_d, ssm_log_dt, ssm_w_glu):
    cond = jax.nn.silu(c)
    for l in range(DEPTH):
        mod = (cond @ ada_w[l] + ada_b[l]).reshape(-1, N_SUB, 3, D_MODEL)
        i = l // 2

        x = sublayer(x, lambda h: swiglu(h, ffn_w_in[l, 0], ffn_w_out[l, 0]),
                     mod[:, 0], norm_pre[l, 0], norm_post[l, 0], FFN_RES_WEIGHT)

        if l % 2 == 0:
            def mix(h):
                z = h @ ab_w_in[i]
                y_a = pool_mixer(z[..., :POOL_WIDTH], pool_w[i], pool_scale[i])
                y_b = sgu_mixer(jax.nn.gelu(z[..., POOL_WIDTH:]), sgu_ln_g[i], sgu_ln_b[i],
                                sgu_w[i], sgu_b[i])
                return jnp.concatenate([y_a, y_b], axis=-1) @ ab_w_out[i]
        else:
            def mix(h):
                return s5_mixer(h @ ssm_w_in[i], ssm_lam_re[i], ssm_lam_im[i],
                                ssm_b_re[i], ssm_b_im[i], ssm_c_re[i], ssm_c_im[i],
                                ssm_d[i], ssm_log_dt[i], ssm_w_glu[i])
        x = sublayer(x, mix, mod[:, 1], norm_pre[l, 1], norm_post[l, 1], 1.0)

        x = sublayer(x, lambda h: swiglu(h, ffn_w_in[l, 1], ffn_w_out[l, 1]),
                     mod[:, 2], norm_pre[l, 2], norm_post[l, 2], FFN_RES_WEIGHT)
    return x
```

```python
import functools
import math

import jax
import jax.numpy as jnp
import numpy as np
from jax import lax
from jax.experimental import pallas as pl
from jax.experimental.pallas import tpu as pltpu

F32 = jnp.float32
BF16 = jnp.bfloat16
EPS = 1e-6
N_SUB = 3
FFN_RES_WEIGHT = 0.5
POOL_WINDOWS = (2, 4, 8, 16)
SGU_HEADS = 4
SGU_CHUNK = 128
SSM_GROUP = 16
SSM_STATE = 64
SSM_CHUNK = 16
LANES = 128
SLAB_GROUPS = LANES // SSM_GROUP
MXU_DIM = 256
VMEM_LIMIT_BYTES = 56 * 1024 * 1024
ROW_TILE = 512


def _const_spec(shape):
    nd = len(shape)
    return pl.BlockSpec(shape, lambda *_: (0,) * nd, pipeline_mode=pl.Buffered(1))


def _params(*sem):
    return pltpu.CompilerParams(dimension_semantics=sem, vmem_limit_bytes=VMEM_LIMIT_BYTES)


def _rms(x, g):
    return x * lax.rsqrt(jnp.mean(x * x, axis=-1, keepdims=True) + EPS) * g


def _pre(x, mod, g_pre):
    return _rms(x, g_pre) * (1.0 + mod[1:2]) + mod[0:1]


def _post(x, y, mod, g_post, res_weight):
    return x + res_weight * mod[2:3] * _rms(y, g_post)


def _sigmoid(v):
    return 1.0 / (1.0 + jnp.exp(-v))


def _dot(a, b):
    return jnp.dot(a, b, preferred_element_type=F32)


def _ada_kernel(c_ref, w_ref, b_ref, o_ref):
    c = c_ref[...]
    cond = (c * _sigmoid(c)).astype(BF16)
    o_ref[...] = _dot(cond, w_ref[...].astype(BF16)) + b_ref[...]


def _ada_mod(c, ada_w, ada_b):
    depth, d, n = ada_w.shape
    bsz = c.shape[0]
    tn = n // 8
    return pl.pallas_call(
        _ada_kernel,
        grid=(depth, n // tn),
        in_specs=[
            pl.BlockSpec((bsz, d), lambda l, j: (0, 0)),
            pl.BlockSpec((None, d, tn), lambda l, j: (l, 0, j)),
            pl.BlockSpec((None, 1, tn), lambda l, j: (l, 0, j)),
        ],
        out_specs=pl.BlockSpec((None, bsz, tn), lambda l, j: (l, 0, j)),
        out_shape=jax.ShapeDtypeStruct((depth, bsz, n), F32),
        compiler_params=_params("arbitrary", "arbitrary"),
    )(c, ada_w, ada_b.reshape(depth, 1, n))


def _ffn_kernel(x_ref, mod_ref, gpre_ref, gpost_ref, win_ref, wout_ref, o_ref, *, d_ff, fc):
    x = x_ref[...]
    mod = mod_ref[...]
    h = _pre(x, mod, gpre_ref[...]).astype(BF16)
    acc = jnp.zeros(x.shape, F32)
    for j in range(d_ff // fc):
        a = _dot(h, win_ref[:, j * fc:(j + 1) * fc])
        b = _dot(h, win_ref[:, d_ff + j * fc:d_ff + (j + 1) * fc])
        act = (a * _sigmoid(a) * b).astype(BF16)
        acc = acc + _dot(act, wout_ref[j * fc:(j + 1) * fc, :])
    o_ref[...] = _post(x, acc, mod, gpost_ref[...], FFN_RES_WEIGHT)


def _row_specs(tm, d):
    x_spec = pl.BlockSpec((None, tm, d), lambda b, i: (b, i, 0))
    mod_spec = pl.BlockSpec((None, 3, d), lambda b, i: (b, 0, 0))
    return x_spec, mod_spec


def _ffn(x, mod, g_pre, g_post, w_in, w_out):
    bsz, s, d = x.shape
    d_ff = w_out.shape[0]
    tm = min(ROW_TILE, s)
    x_spec, mod_spec = _row_specs(tm, d)
    return pl.pallas_call(
        functools.partial(_ffn_kernel, d_ff=d_ff, fc=MXU_DIM),
        grid=(bsz, s // tm),
        in_specs=[x_spec, mod_spec, _const_spec((1, d)), _const_spec((1, d)),
                  _const_spec(w_in.shape), _const_spec(w_out.shape)],
        out_specs=x_spec,
        out_shape=jax.ShapeDtypeStruct(x.shape, F32),
        compiler_params=_params("arbitrary", "arbitrary"),
    )(x, mod, g_pre.reshape(1, d), g_post.reshape(1, d), w_in.astype(BF16), w_out.astype(BF16))


def _mix0_kernel(x_ref, mod_ref, gpre_ref, gpost_ref, win_ref, poolw_ref, pscale_ref, lng_ref, lnb_ref,
                 sguw_ref, sgub_ref, wout_ref, o_ref, carry_ref, *, pool_width):
    i = pl.program_id(1)
    tm = x_ref.shape[0]
    n_pool = len(POOL_WINDOWS)
    gd = pool_width // n_pool

    @pl.when(i == 0)
    def _():
        carry_ref[...] = jnp.zeros(carry_ref.shape, F32)

    x = x_ref[...]
    mod = mod_ref[...]
    h = _pre(x, mod, gpre_ref[...]).astype(BF16)
    z = _dot(h, win_ref[...])

    a = z[:, :pool_width]
    row8 = lax.broadcasted_iota(jnp.int32, (8, 1), 0)

    def shifted(cur, level, k):
        lanes = cur.shape[1]
        prev8 = carry_ref[level, :, pool_width - lanes:]
        carry_ref[level, :, pool_width - lanes:] = cur[tm - 8:, :]
        rolled = pltpu.roll(cur, k, axis=0)
        if k == 8:
            return jnp.concatenate([prev8, rolled[8:]], axis=0)
        head = jnp.where(row8 < k, pltpu.roll(prev8, k, axis=0), rolled[:8])
        return jnp.concatenate([head, rolled[8:]], axis=0)

    pos = (i * tm + 1 + lax.broadcasted_iota(jnp.int32, (tm, 1), 0)).astype(F32)
    level_sum = a
    y_a = []
    for g, w in enumerate(POOL_WINDOWS):
        level_sum = level_sum + shifted(level_sum, g, w // 2)
        mean = level_sum[:, :gd] / jnp.minimum(pos, float(w))
        dg = (mean - a[:, g * gd:(g + 1) * gd]).astype(BF16)
        y_a.append(_dot(dg, poolw_ref[g]) * pscale_ref[:, g * gd:(g + 1) * gd])
        level_sum = level_sum[:, gd:]
    y_a = jnp.concatenate(y_a, axis=-1).astype(BF16)

    zb = jax.nn.gelu(z[:, pool_width:])
    sgu_width = zb.shape[1] // 2
    hd = sgu_width // SGU_HEADS
    u = zb[:, :sgu_width]
    v = zb[:, sgu_width:]
    y_b = []
    for hh in range(SGU_HEADS):
        vh = v[:, hh * hd:(hh + 1) * hd]
        mu = jnp.mean(vh, axis=-1, keepdims=True)
        var = jnp.mean(jnp.square(vh - mu), axis=-1, keepdims=True)
        vn = (vh - mu) * lax.rsqrt(var + EPS) * lng_ref[:, hh * hd:(hh + 1) * hd] + lnb_ref[:, hh * hd:(hh + 1) * hd]
        vn = vn.astype(BF16)
        s_h = [_dot(sguw_ref[hh], vn[ck * SGU_CHUNK:(ck + 1) * SGU_CHUNK]) + sgub_ref[hh]
               for ck in range(tm // SGU_CHUNK)]
        y_b.append(u[:, hh * hd:(hh + 1) * hd] * jnp.concatenate(s_h, axis=0))
    y_b = jnp.concatenate(y_b, axis=-1).astype(BF16)

    y = _dot(y_a, wout_ref[:pool_width, :]) + _dot(y_b, wout_ref[pool_width:, :])
    o_ref[...] = _post(x, y, mod, gpost_ref[...], 1.0)


def _mix0(x, mod, g_pre, g_post, w_in, pool_w, pool_scale, ln_g, ln_b, sgu_w, sgu_b, w_out):
    bsz, s, d = x.shape
    pool_width = pool_scale.shape[0]
    sgu_width = ln_g.shape[0]
    hd = sgu_width // SGU_HEADS
    tm = min(ROW_TILE, s)
    x_spec, mod_spec = _row_specs(tm, d)
    causal = jnp.tril(jnp.ones((SGU_CHUNK, SGU_CHUNK), dtype=bool))
    sgu_w_causal = jnp.where(causal[None], sgu_w, 0.0).astype(BF16)
    sgu_b_rows = jnp.broadcast_to(sgu_b[:, :, None], (SGU_HEADS, SGU_CHUNK, hd))
    return pl.pallas_call(
        functools.partial(_mix0_kernel, pool_width=pool_width),
        grid=(bsz, s // tm),
        in_specs=[x_spec, mod_spec, _const_spec((1, d)), _const_spec((1, d)),
                  _const_spec(w_in.shape), _const_spec(pool_w.shape), _const_spec((1, pool_width)),
                  _const_spec((1, sgu_width)), _const_spec((1, sgu_width)),
                  _const_spec(sgu_w.shape), _const_spec(sgu_b_rows.shape), _const_spec(w_out.shape)],
        out_specs=x_spec,
        out_shape=jax.ShapeDtypeStruct(x.shape, F32),
        scratch_shapes=[pltpu.VMEM((len(POOL_WINDOWS), 8, pool_width), F32)],
        compiler_params=_params("arbitrary", "arbitrary"),
    )(x, mod, g_pre.reshape(1, d), g_post.reshape(1, d), w_in.astype(BF16), pool_w.astype(BF16),
      pool_scale.reshape(1, pool_width), ln_g.reshape(1, sgu_width), ln_b.reshape(1, sgu_width),
      sgu_w_causal, sgu_b_rows, w_out.astype(BF16))


def _s5_in_kernel(x_ref, mod_ref, gpre_ref, w_ref, o_ref, u_ref):
    n_slab, rows, _ = o_ref.shape
    h = _pre(x_ref[...], mod_ref[...], gpre_ref[...]).astype(BF16)
    u = _dot(h, w_ref[...])
    for j in range(n_slab):
        u_ref[j] = u[:, j * LANES:(j + 1) * LANES]
    for l in range(SSM_CHUNK):
        for j in range(n_slab):
            o_ref[j, :, l * LANES:(l + 1) * LANES] = (
                u_ref[j, pl.ds(l, rows, stride=SSM_CHUNK), :].astype(BF16))


def _s5_in(x, mod, g_pre, w_in):
    bsz, s, d = x.shape
    width = w_in.shape[1]
    n_slab = width // LANES
    tm = min(ROW_TILE, s)
    rows = tm // SSM_CHUNK
    tiles = s // tm
    x_spec, mod_spec = _row_specs(tm, d)
    return pl.pallas_call(
        _s5_in_kernel,
        grid=(bsz, tiles),
        in_specs=[x_spec, mod_spec, _const_spec((1, d)), _const_spec(w_in.shape)],
        out_specs=pl.BlockSpec((n_slab, rows, SSM_CHUNK * LANES), lambda b, i: (0, b * tiles + i, 0)),
        out_shape=jax.ShapeDtypeStruct((n_slab, bsz * s // SSM_CHUNK, SSM_CHUNK * LANES), BF16),
        scratch_shapes=[pltpu.VMEM((n_slab, tm, LANES), F32)],
        compiler_params=_params("arbitrary", "arbitrary"),
    )(x, mod, g_pre.reshape(1, d), w_in.astype(BF16))


def _s5_core_kernel(u_ref, toep_ref, p_ref, q_ref, a_ref, d_ref, o_ref, st_ref, *, bsz, n_chunk):
    n_lt = st_ref.shape[0]
    half = n_lt // 2
    u = u_ref[...]

    z = _dot(u, p_ref[...])
    for k in range(n_lt):
        st_ref[k] = z[:, k * LANES:(k + 1) * LANES]
    a_re = [a_ref[0:1, k * LANES:(k + 1) * LANES] for k in range(half)]
    a_im = [a_ref[1:2, k * LANES:(k + 1) * LANES] for k in range(half)]
    s_re = [jnp.zeros((bsz, LANES), F32)] * half
    s_im = [jnp.zeros((bsz, LANES), F32)] * half
    for c in range(n_chunk):
        rows = pl.ds(c, bsz, stride=n_chunk)
        for k in range(half):
            z_re = st_ref[k, rows, :]
            z_im = st_ref[half + k, rows, :]
            st_ref[k, rows, :] = s_re[k]
            st_ref[half + k, rows, :] = s_im[k]
            s_re[k], s_im[k] = (a_re[k] * s_re[k] - a_im[k] * s_im[k] + z_re,
                                a_re[k] * s_im[k] + a_im[k] * s_re[k] + z_im)
    st = jnp.concatenate([st_ref[k].astype(BF16) for k in range(n_lt)], axis=-1)

    n_tile = u.shape[1] // MXU_DIM
    for n in range(n_tile):
        cols = slice(n * MXU_DIM, (n + 1) * MXU_DIM)
        y = _dot(u[:, :(n + 1) * MXU_DIM], toep_ref[(n_tile - 1 - n) * MXU_DIM:, :])
        y = y + _dot(st, q_ref[:, cols]) + d_ref[:, cols] * u[:, cols].astype(F32)
        o_ref[:, cols] = jax.nn.gelu(y).astype(BF16)


def _s5_core(u_cm, toep, p_mat, q_mat, a_chunk, d_cm, bsz):
    n_slab, rows, kdim = u_cm.shape
    n_state = p_mat.shape[2]
    slab = lambda shape: pl.BlockSpec((None,) + shape, lambda j: (j, 0, 0))
    return pl.pallas_call(
        functools.partial(_s5_core_kernel, bsz=bsz, n_chunk=rows // bsz),
        grid=(n_slab,),
        in_specs=[slab((rows, kdim)), slab(toep.shape[1:]), slab(p_mat.shape[1:]), slab(q_mat.shape[1:]),
                  slab(a_chunk.shape[1:]), slab(d_cm.shape[1:])],
        out_specs=slab((rows, kdim)),
        out_shape=jax.ShapeDtypeStruct(u_cm.shape, BF16),
        scratch_shapes=[pltpu.VMEM((n_state // LANES, rows, LANES), F32)],
        compiler_params=_params("arbitrary"),
    )(u_cm, toep, p_mat, q_mat, a_chunk, d_cm)


def _s5_out_kernel(x_ref, mod_ref, gpost_ref, g_ref, w_ref, o_ref, g_nat_ref):
    n_slab, rows, _ = g_ref.shape
    for l in range(SSM_CHUNK):
        for j in range(n_slab):
            g_nat_ref[j, pl.ds(l, rows, stride=SSM_CHUNK), :] = (
                g_ref[j, :, l * LANES:(l + 1) * LANES].astype(F32))
    g = jnp.concatenate([g_nat_ref[j].astype(BF16) for j in range(n_slab)], axis=-1)
    ab = _dot(g, w_ref[...])
    d = ab.shape[1] // 2
    y = ab[:, :d] * _sigmoid(ab[:, d:])
    o_ref[...] = _post(x_ref[...], y, mod_ref[...], gpost_ref[...], 1.0)


def _s5_out(x, mod, g_post, g_cm, w_glu):
    bsz, s, d = x.shape
    n_slab = g_cm.shape[0]
    tm = min(ROW_TILE, s)
    rows = tm // SSM_CHUNK
    tiles = s // tm
    x_spec, mod_spec = _row_specs(tm, d)
    return pl.pallas_call(
        _s5_out_kernel,
        grid=(bsz, tiles),
        in_specs=[x_spec, mod_spec, _const_spec((1, d)),
                  pl.BlockSpec((n_slab, rows, SSM_CHUNK * LANES), lambda b, i: (0, b * tiles + i, 0)),
                  _const_spec(w_glu.shape)],
        out_specs=x_spec,
        out_shape=jax.ShapeDtypeStruct(x.shape, F32),
        scratch_shapes=[pltpu.VMEM((n_slab, tm, LANES), F32)],
        compiler_params=_params("arbitrary", "arbitrary"),
    )(x, mod, g_post.reshape(1, d), g_cm, w_glu.astype(BF16))


def _s5_operators(lam_re, lam_im, b_re, b_im, c_re, c_im, d_skip, log_dt):
    n_group, n_state = lam_re.shape
    n_ch = b_re.shape[2]
    n_slab = n_group // SLAB_GROUPS
    hi = lax.Precision.HIGHEST
    lam = lax.complex(lam_re, lam_im)
    dt = jnp.exp(log_dt)[:, None]
    lam_bar = jnp.exp(lam * dt)
    b_bar = ((lam_bar - 1.0) / lam)[..., None] * lax.complex(b_re, b_im)
    cmat = lax.complex(c_re, c_im)
    tau = jnp.arange(SSM_CHUNK + 1, dtype=F32)[:, None, None]
    apow = jnp.exp((lam * dt)[None] * tau)

    ca = cmat[:, None] * apow[:SSM_CHUNK].transpose(1, 0, 2)[:, :, None, :]
    kern = (jnp.einsum('gtmp,gpn->gtmn', ca.real, b_bar.real, precision=hi)
            - jnp.einsum('gtmp,gpn->gtmn', ca.imag, b_bar.imag, precision=hi))
    kern = jnp.concatenate([jnp.zeros_like(kern[:, :1]), kern], axis=1)
    l_in = np.arange(SSM_CHUNK)[:, None]
    r_out = np.arange(2)[None, :]
    lag = (SSM_CHUNK - 2) + r_out - l_in + 1
    blocks = kern[:, lag]
    blocks = blocks.reshape(n_slab, SLAB_GROUPS, SSM_CHUNK, 2, n_ch, n_ch)
    eye = jnp.eye(SLAB_GROUPS, dtype=F32)
    toep = jnp.einsum('ab,jalrmn->jlanrbm', eye, blocks)
    toep = toep.reshape(n_slab, SSM_CHUNK * LANES, 2 * LANES)

    apow_rev = jnp.exp((lam * dt)[None] * (SSM_CHUNK - 1.0 - tau[:SSM_CHUNK]))
    pin = apow_rev.transpose(1, 0, 2)[..., None] * b_bar[:, None]
    pin = pin.reshape(n_slab, SLAB_GROUPS, SSM_CHUNK, n_state, n_ch)
    p_mat = jnp.concatenate([jnp.einsum('ab,jalpn->jlanbp', eye, part).reshape(
        n_slab, SSM_CHUNK * LANES, SLAB_GROUPS * n_state) for part in (pin.real, pin.imag)], axis=-1)

    qout = cmat[:, None] * apow[1:].transpose(1, 0, 2)[:, :, None, :]
    qout = qout.reshape(n_slab, SLAB_GROUPS, SSM_CHUNK, n_ch, n_state)
    q_mat = jnp.concatenate([jnp.einsum('ab,jalmp->japlbm', eye, part).reshape(
        n_slab, SLAB_GROUPS * n_state, SSM_CHUNK * LANES) for part in (qout.real, -qout.imag)], axis=1)

    a_l = apow[SSM_CHUNK].reshape(n_slab, 1, SLAB_GROUPS * n_state)
    a_chunk = jnp.concatenate([a_l.real, a_l.imag], axis=1)
    d_cm = jnp.tile(d_skip.reshape(n_slab, 1, LANES), (1, 1, SSM_CHUNK))
    return toep.astype(BF16), p_mat.astype(BF16), q_mat.astype(BF16), a_chunk, d_cm


def kernel(x, c, ada_w, ada_b, norm_pre, norm_post, ffn_w_in, ffn_w_out, ab_w_in, pool_w, pool_scale, sgu_ln_g, sgu_ln_b, sgu_w, sgu_b, ab_w_out, ssm_w_in, ssm_lam_re, ssm_lam_im, ssm_b_re, ssm_b_im, ssm_c_re, ssm_c_im, ssm_d, ssm_log_dt, ssm_w_glu):
    bsz, s, d = x.shape
    depth = ada_w.shape[0]
    assert s % ROW_TILE == 0 or s < ROW_TILE
    mod_all = _ada_mod(c, ada_w, ada_b).reshape(depth, bsz, N_SUB, 3, d)
    for l in range(depth):
        i = l // 2
        mod = mod_all[l]
        x = _ffn(x, mod[:, 0], norm_pre[l, 0], norm_post[l, 0], ffn_w_in[l, 0], ffn_w_out[l, 0])
        if l % 2 == 0:
            x = _mix0(x, mod[:, 1], norm_pre[l, 1], norm_post[l, 1], ab_w_in[i], pool_w[i], pool_scale[i],
                      sgu_ln_g[i], sgu_ln_b[i], sgu_w[i], sgu_b[i], ab_w_out[i])
        else:
            ops = _s5_operators(ssm_lam_re[i], ssm_lam_im[i], ssm_b_re[i], ssm_b_im[i], ssm_c_re[i],
                                ssm_c_im[i], ssm_d[i], ssm_log_dt[i])
            u_cm = _s5_in(x, mod[:, 1], norm_pre[l, 1], ssm_w_in[i])
            g_cm = _s5_core(u_cm, *ops, bsz)
            x = _s5_out(x, mod[:, 1], norm_post[l, 1], g_cm, ssm_w_glu[i])
        x = _ffn(x, mod[:, 2], norm_pre[l, 2], norm_post[l, 2], ffn_w_in[l, 1], ffn_w_out[l, 1])
    return x
```

```python
import functools

import jax
import jax.numpy as jnp
from jax import lax
from jax.experimental import pallas as pl
from jax.experimental.pallas import tpu as pltpu

F32 = jnp.float32
BF16 = jnp.bfloat16
EPS = 1e-6
N_SUB = 3
FFN_RES_WEIGHT = 0.5
POOL_WINDOWS = (2, 4, 8, 16)
SGU_HEADS = 4
SGU_CHUNK = 128
SSM_GROUP = 16
SSM_CHUNK = 16
LANES = 128
SLAB_GROUPS = LANES // SSM_GROUP
MXU_DIM = 256
VMEM_LIMIT_BYTES = 56 * 1024 * 1024
ROW_TILE = 512


def _const_spec(shape, lead=()):
    nd = len(shape)
    return pl.BlockSpec((None,) * len(lead) + tuple(shape), lambda *_: tuple(lead) + (0,) * nd,
                        pipeline_mode=pl.Buffered(1))


def _params(*sem):
    return pltpu.CompilerParams(dimension_semantics=sem, vmem_limit_bytes=VMEM_LIMIT_BYTES)


def _rms(x, g):
    return x * lax.rsqrt(jnp.mean(x * x, axis=-1, keepdims=True) + EPS) * g


def _pre(x, mod, g_pre):
    return _rms(x, g_pre) * (1.0 + mod[1:2]) + mod[0:1]


def _post(x, y, mod, g_post, res_weight):
    return x + res_weight * mod[2:3] * _rms(y, g_post)


def _sigmoid(v):
    return 1.0 / (1.0 + jnp.exp(-v))


def _dot(a, b):
    return jnp.dot(a, b, preferred_element_type=F32)


def _ada_kernel(c_ref, w_ref, b_ref, o_ref):
    c = c_ref[...]
    cond = (c * _sigmoid(c)).astype(BF16)
    o_ref[...] = _dot(cond, w_ref[...].astype(BF16)) + b_ref[...]


def _ada_mod(c, ada_w, ada_b):
    depth, d, n = ada_w.shape
    bsz = c.shape[0]
    tn = n // 8
    return pl.pallas_call(
        _ada_kernel,
        grid=(depth, n // tn),
        in_specs=[
            pl.BlockSpec((bsz, d), lambda l, j: (0, 0)),
            pl.BlockSpec((None, d, tn), lambda l, j: (l, 0, j)),
            pl.BlockSpec((None, 1, tn), lambda l, j: (l, 0, j)),
        ],
        out_specs=pl.BlockSpec((None, bsz, tn), lambda l, j: (l, 0, j)),
        out_shape=jax.ShapeDtypeStruct((depth, bsz, n), F32),
        compiler_params=_params("arbitrary", "arbitrary"),
    )(c, ada_w, ada_b.reshape(depth, 1, n))


def _x_spec(tm, d):
    return pl.BlockSpec((None, tm, d), lambda b, i: (b, i, 0))


def _mod_spec(sub, d):
    l, k = sub
    return pl.BlockSpec((None, None, None, 3, d), lambda b, i: (l, b, k, 0, 0))


def _norm_spec(sub, d):
    return _const_spec((1, d), lead=sub)


def _ffn_kernel(x_ref, mod_ref, gpre_ref, gpost_ref, win_ref, wout_ref, o_ref, *, d_ff, fc):
    x = x_ref[...]
    mod = mod_ref[...]
    h = _pre(x, mod, gpre_ref[...]).astype(BF16)
    acc = jnp.zeros(x.shape, F32)
    for j in range(d_ff // fc):
        a = _dot(h, win_ref[:, j * fc:(j + 1) * fc])
        b = _dot(h, win_ref[:, d_ff + j * fc:d_ff + (j + 1) * fc])
        act = (a * _sigmoid(a) * b).astype(BF16)
        acc = acc + _dot(act, wout_ref[j * fc:(j + 1) * fc, :])
    o_ref[...] = _post(x, acc, mod, gpost_ref[...], FFN_RES_WEIGHT)


def _ffn(x, mod, norm_pre, norm_post, sub, w_in, w_out, half):
    bsz, s, d = x.shape
    d_ff = w_out.shape[2]
    tm = min(ROW_TILE, s)
    lead = (sub[0], half)
    return pl.pallas_call(
        functools.partial(_ffn_kernel, d_ff=d_ff, fc=MXU_DIM),
        grid=(bsz, s // tm),
        in_specs=[_x_spec(tm, d), _mod_spec(sub, d), _norm_spec(sub, d), _norm_spec(sub, d),
                  _const_spec(w_in.shape[2:], lead), _const_spec(w_out.shape[2:], lead)],
        out_specs=_x_spec(tm, d),
        out_shape=jax.ShapeDtypeStruct(x.shape, F32),
        compiler_params=_params("arbitrary", "arbitrary"),
    )(x, mod, norm_pre, norm_post, w_in, w_out)


def _mix0_kernel(x_ref, mod_ref, gpre_ref, gpost_ref, win_ref, poolw_ref, pscale_ref, lng_ref, lnb_ref,
                 sguw_ref, sgub_ref, wout_ref, o_ref, carry_ref, *, pool_width):
    i = pl.program_id(1)
    tm = x_ref.shape[0]
    n_pool = len(POOL_WINDOWS)
    gd = pool_width // n_pool

    @pl.when(i == 0)
    def _():
        carry_ref[...] = jnp.zeros(carry_ref.shape, F32)

    x = x_ref[...]
    mod = mod_ref[...]
    h = _pre(x, mod, gpre_ref[...]).astype(BF16)
    z = _dot(h, win_ref[...])

    a = z[:, :pool_width]
    row8 = lax.broadcasted_iota(jnp.int32, (8, 1), 0)

    def shifted(cur, level, k):
        lanes = cur.shape[1]
        prev8 = carry_ref[level, :, pool_width - lanes:]
        carry_ref[level, :, pool_width - lanes:] = cur[tm - 8:, :]
        rolled = pltpu.roll(cur, k, axis=0)
        if k == 8:
            return jnp.concatenate([prev8, rolled[8:]], axis=0)
        head = jnp.where(row8 < k, pltpu.roll(prev8, k, axis=0), rolled[:8])
        return jnp.concatenate([head, rolled[8:]], axis=0)

    pos = (i * tm + 1 + lax.broadcasted_iota(jnp.int32, (tm, 1), 0)).astype(F32)
    level_sum = a
    y_a = []
    for g, w in enumerate(POOL_WINDOWS):
        level_sum = level_sum + shifted(level_sum, g, w // 2)
        mean = level_sum[:, :gd] / jnp.minimum(pos, float(w))
        dg = (mean - a[:, g * gd:(g + 1) * gd]).astype(BF16)
        y_a.append(_dot(dg, poolw_ref[g]) * pscale_ref[:, g * gd:(g + 1) * gd])
        level_sum = level_sum[:, gd:]
    y_a = jnp.concatenate(y_a, axis=-1).astype(BF16)

    zb = jax.nn.gelu(z[:, pool_width:])
    sgu_width = zb.shape[1] // 2
    hd = sgu_width // SGU_HEADS
    u = zb[:, :sgu_width]
    v = zb[:, sgu_width:]
    y_b = []
    for hh in range(SGU_HEADS):
        vh = v[:, hh * hd:(hh + 1) * hd]
        mu = jnp.mean(vh, axis=-1, keepdims=True)
        var = jnp.mean(jnp.square(vh - mu), axis=-1, keepdims=True)
        vn = (vh - mu) * lax.rsqrt(var + EPS) * lng_ref[:, hh * hd:(hh + 1) * hd] + lnb_ref[:, hh * hd:(hh + 1) * hd]
        vn = vn.astype(BF16)
        s_h = [_dot(sguw_ref[hh], vn[ck * SGU_CHUNK:(ck + 1) * SGU_CHUNK]) + sgub_ref[hh]
               for ck in range(tm // SGU_CHUNK)]
        y_b.append(u[:, hh * hd:(hh + 1) * hd] * jnp.concatenate(s_h, axis=0))
    y_b = jnp.concatenate(y_b, axis=-1).astype(BF16)

    y = _dot(y_a, wout_ref[:pool_width, :]) + _dot(y_b, wout_ref[pool_width:, :])
    o_ref[...] = _post(x, y, mod, gpost_ref[...], 1.0)


def _mix0(x, mod, norm_pre, norm_post, sub, w_in, pool_w, pool_scale, ln_g, ln_b, sgu_w, sgu_b, w_out):
    bsz, s, d = x.shape
    pool_width = pool_scale.shape[0]
    sgu_width = ln_g.shape[0]
    hd = sgu_width // SGU_HEADS
    tm = min(ROW_TILE, s)
    causal = jnp.tril(jnp.ones((SGU_CHUNK, SGU_CHUNK), dtype=bool))
    sgu_w_causal = jnp.where(causal[None], sgu_w, 0.0).astype(BF16)
    sgu_b_rows = jnp.broadcast_to(sgu_b[:, :, None], (SGU_HEADS, SGU_CHUNK, hd))
    return pl.pallas_call(
        functools.partial(_mix0_kernel, pool_width=pool_width),
        grid=(bsz, s // tm),
        in_specs=[_x_spec(tm, d), _mod_spec(sub, d), _norm_spec(sub, d), _norm_spec(sub, d),
                  _const_spec(w_in.shape), _const_spec(pool_w.shape), _const_spec((1, pool_width)),
                  _const_spec((1, sgu_width)), _const_spec((1, sgu_width)),
                  _const_spec(sgu_w.shape), _const_spec(sgu_b_rows.shape), _const_spec(w_out.shape)],
        out_specs=_x_spec(tm, d),
        out_shape=jax.ShapeDtypeStruct(x.shape, F32),
        scratch_shapes=[pltpu.VMEM((len(POOL_WINDOWS), 8, pool_width), F32)],
        compiler_params=_params("arbitrary", "arbitrary"),
    )(x, mod, norm_pre, norm_post, w_in.astype(BF16), pool_w.astype(BF16),
      pool_scale.reshape(1, pool_width), ln_g.reshape(1, sgu_width), ln_b.reshape(1, sgu_width),
      sgu_w_causal, sgu_b_rows, w_out.astype(BF16))


def _s5_in_kernel(x_ref, mod_ref, gpre_ref, w_ref, o_ref, u_ref):
    n_slab, rows, _ = o_ref.shape
    h = _pre(x_ref[...], mod_ref[...], gpre_ref[...]).astype(BF16)
    u = _dot(h, w_ref[...])
    for j in range(n_slab):
        u_ref[j] = u[:, j * LANES:(j + 1) * LANES]
    for l in range(SSM_CHUNK):
        for j in range(n_slab):
            o_ref[j, :, l * LANES:(l + 1) * LANES] = (
                u_ref[j, pl.ds(l, rows, stride=SSM_CHUNK), :].astype(BF16))


def _s5_in(x, mod, norm_pre, sub, w_in):
    bsz, s, d = x.shape
    width = w_in.shape[1]
    n_slab = width // LANES
    tm = min(ROW_TILE, s)
    rows = tm // SSM_CHUNK
    tiles = s // tm
    return pl.pallas_call(
        _s5_in_kernel,
        grid=(bsz, tiles),
        in_specs=[_x_spec(tm, d), _mod_spec(sub, d), _norm_spec(sub, d), _const_spec(w_in.shape)],
        out_specs=pl.BlockSpec((n_slab, rows, SSM_CHUNK * LANES), lambda b, i: (0, b * tiles + i, 0)),
        out_shape=jax.ShapeDtypeStruct((n_slab, bsz * s // SSM_CHUNK, SSM_CHUNK * LANES), BF16),
        scratch_shapes=[pltpu.VMEM((n_slab, tm, LANES), F32)],
        compiler_params=_params("arbitrary", "arbitrary"),
    )(x, mod, norm_pre, w_in.astype(BF16))


def _s5_core_kernel(u_ref, lam_ref, bt_ref, ct_ref, d_ref, o_ref, st_ref, p_ref, qt_ref, toep_ref, *,
                    bsz, n_chunk):
    n_lt = st_ref.shape[0]
    half = n_lt // 2
    n_st = half * LANES
    u = u_ref[...]

    lam_re, lam_im = lam_ref[0:1], lam_ref[1:2]
    dt = jnp.exp(lam_ref[2:3])
    tau = lax.broadcasted_iota(jnp.int32, (SSM_CHUNK + 8, 1), 0).astype(F32)
    mag = jnp.exp(lam_re * dt * tau)
    ang = lam_im * dt * tau
    ap_re, ap_im = mag * jnp.cos(ang), mag * jnp.sin(ang)
    num_re, num_im = ap_re[1:2] - 1.0, ap_im[1:2]
    den = lam_re * lam_re + lam_im * lam_im
    cf_re = (num_re * lam_re + num_im * lam_im) / den
    cf_im = (num_im * lam_re - num_re * lam_im) / den
    bt_re, bt_im = bt_ref[0], bt_ref[1]
    bb_re, bb_im = cf_re * bt_re - cf_im * bt_im, cf_re * bt_im + cf_im * bt_re
    ct_re, ct_im = ct_ref[0], ct_ref[1]

    nt = (((1,), (1,)), ((), ()))
    for l in range(SSM_CHUNK):
        blk = slice(l * LANES, (l + 1) * LANES)
        a_r, a_i = ap_re[SSM_CHUNK - 1 - l:SSM_CHUNK - l], ap_im[SSM_CHUNK - 1 - l:SSM_CHUNK - l]
        p_r, p_i = a_r * bb_re - a_i * bb_im, a_r * bb_im + a_i * bb_re
        p_ref[blk, :n_st] = p_r.astype(BF16)
        p_ref[blk, n_st:] = p_i.astype(BF16)
        imp = (lax.dot_general(p_r, ct_re, nt, precision=lax.Precision.HIGHEST, preferred_element_type=F32)
               - lax.dot_general(p_i, ct_im, nt, precision=lax.Precision.HIGHEST, preferred_element_type=F32))
        toep_ref[blk, LANES:] = imp.astype(BF16)
        if l > 0:
            toep_ref[(l - 1) * LANES:l * LANES, :LANES] = imp.astype(BF16)
        a_r, a_i = ap_re[l + 1:l + 2], ap_im[l + 1:l + 2]
        qt_ref[blk, :n_st] = (a_r * ct_re - a_i * ct_im).astype(BF16)
        qt_ref[blk, n_st:] = (-(a_r * ct_im + a_i * ct_re)).astype(BF16)
    toep_ref[(SSM_CHUNK - 1) * LANES:, :LANES] = jnp.zeros((LANES, LANES), BF16)

    z = _dot(u, p_ref[...])
    for k in range(n_lt):
        st_ref[k] = z[:, k * LANES:(k + 1) * LANES]
    a_re = [ap_re[SSM_CHUNK:SSM_CHUNK + 1, k * LANES:(k + 1) * LANES] for k in range(half)]
    a_im = [ap_im[SSM_CHUNK:SSM_CHUNK + 1, k * LANES:(k + 1) * LANES] for k in range(half)]
    s_re = [jnp.zeros((bsz, LANES), F32)] * half
    s_im = [jnp.zeros((bsz, LANES), F32)] * half
    for c in range(n_chunk):
        rows = pl.ds(c, bsz, stride=n_chunk)
        for k in range(half):
            z_re = st_ref[k, rows, :]
            z_im = st_ref[half + k, rows, :]
            st_ref[k, rows, :] = s_re[k]
            st_ref[half + k, rows, :] = s_im[k]
            s_re[k], s_im[k] = (a_re[k] * s_re[k] - a_im[k] * s_im[k] + z_re,
                                a_re[k] * s_im[k] + a_im[k] * s_re[k] + z_im)
    st = jnp.concatenate([st_ref[k].astype(BF16) for k in range(n_lt)], axis=-1)

    n_tile = u.shape[1] // MXU_DIM
    for n in range(n_tile):
        cols = slice(n * MXU_DIM, (n + 1) * MXU_DIM)
        y = _dot(u[:, :(n + 1) * MXU_DIM], toep_ref[(n_tile - 1 - n) * MXU_DIM:, :])
        y = y + lax.dot_general(st, qt_ref[cols, :], nt, preferred_element_type=F32)
        y = y + d_ref[:, cols] * u[:, cols].astype(F32)
        o_ref[:, cols] = jax.nn.gelu(y).astype(BF16)


def _s5_core(u_cm, lam_rows, bt, ct, d_cm, bsz):
    n_slab, rows, kdim = u_cm.shape
    n_st = lam_rows.shape[2]
    slab = lambda shape: pl.BlockSpec((None,) + shape, lambda j: (j,) + (0,) * len(shape))
    return pl.pallas_call(
        functools.partial(_s5_core_kernel, bsz=bsz, n_chunk=rows // bsz),
        grid=(n_slab,),
        in_specs=[slab((rows, kdim)), slab(lam_rows.shape[1:]), slab(bt.shape[1:]), slab(ct.shape[1:]),
                  slab(d_cm.shape[1:])],
        out_specs=slab((rows, kdim)),
        out_shape=jax.ShapeDtypeStruct(u_cm.shape, BF16),
        scratch_shapes=[pltpu.VMEM((2 * n_st // LANES, rows, LANES), F32),
                        pltpu.VMEM((kdim, 2 * n_st), BF16),
                        pltpu.VMEM((kdim, 2 * n_st), BF16),
                        pltpu.VMEM((kdim, 2 * LANES), BF16)],
        compiler_params=_params("arbitrary"),
    )(u_cm, lam_rows, bt, ct, d_cm)


def _s5_out_kernel(x_ref, mod_ref, gpost_ref, g_ref, w_ref, o_ref, g_nat_ref):
    n_slab, rows, _ = g_ref.shape
    for l in range(SSM_CHUNK):
        for j in range(n_slab):
            g_nat_ref[j, pl.ds(l, rows, stride=SSM_CHUNK), :] = (
                g_ref[j, :, l * LANES:(l + 1) * LANES].astype(F32))
    g = jnp.concatenate([g_nat_ref[j].astype(BF16) for j in range(n_slab)], axis=-1)
    ab = _dot(g, w_ref[...])
    d = ab.shape[1] // 2
    y = ab[:, :d] * _sigmoid(ab[:, d:])
    o_ref[...] = _post(x_ref[...], y, mod_ref[...], gpost_ref[...], 1.0)


def _s5_out(x, mod, norm_post, sub, g_cm, w_glu):
    bsz, s, d = x.shape
    n_slab = g_cm.shape[0]
    tm = min(ROW_TILE, s)
    rows = tm // SSM_CHUNK
    tiles = s // tm
    return pl.pallas_call(
        _s5_out_kernel,
        grid=(bsz, tiles),
        in_specs=[_x_spec(tm, d), _mod_spec(sub, d), _norm_spec(sub, d),
                  pl.BlockSpec((n_slab, rows, SSM_CHUNK * LANES), lambda b, i: (0, b * tiles + i, 0)),
                  _const_spec(w_glu.shape)],
        out_specs=_x_spec(tm, d),
        out_shape=jax.ShapeDtypeStruct(x.shape, F32),
        scratch_shapes=[pltpu.VMEM((n_slab, tm, LANES), F32)],
        compiler_params=_params("arbitrary", "arbitrary"),
    )(x, mod, norm_post, g_cm, w_glu.astype(BF16))


def _s5_slab_params(lam_re, lam_im, b_re, b_im, c_re, c_im, d_skip, log_dt):
    n_group, n_state = lam_re.shape
    n_ch = b_re.shape[2]
    n_slab = n_group // SLAB_GROUPS
    n_st = SLAB_GROUPS * n_state
    log_dt_st = jnp.broadcast_to(log_dt[:, None], (n_group, n_state))
    lam_rows = jnp.stack([v.reshape(n_slab, n_st) for v in (lam_re, lam_im, log_dt_st)], axis=1)
    same_group = jnp.eye(SLAB_GROUPS, dtype=bool)[None, :, None, :, None]

    def block_diag(w_re, w_im):
        parts = []
        for w in (w_re, w_im):
            w = w.reshape(n_slab, SLAB_GROUPS, n_ch, 1, n_state)
            parts.append(jnp.where(same_group, w, 0.0).reshape(n_slab, LANES, n_st))
        return jnp.stack(parts, axis=1)

    bt = block_diag(b_re.transpose(0, 2, 1), b_im.transpose(0, 2, 1))
    ct = block_diag(c_re, c_im)
    d_cm = jnp.tile(d_skip.reshape(n_slab, 1, LANES), (1, 1, SSM_CHUNK))
    return lam_rows, bt, ct, d_cm


def kernel(x, c, ada_w, ada_b, norm_pre, norm_post, ffn_w_in, ffn_w_out, ab_w_in, pool_w, pool_scale, sgu_ln_g, sgu_ln_b, sgu_w, sgu_b, ab_w_out, ssm_w_in, ssm_lam_re, ssm_lam_im, ssm_b_re, ssm_b_im, ssm_c_re, ssm_c_im, ssm_d, ssm_log_dt, ssm_w_glu):
    bsz, s, d = x.shape
    depth = ada_w.shape[0]
    assert s % ROW_TILE == 0 or s < ROW_TILE
    mod = _ada_mod(c, ada_w, ada_b).reshape(depth, bsz, N_SUB, 3, d)
    norm_pre = norm_pre.reshape(depth, N_SUB, 1, d)
    norm_post = norm_post.reshape(depth, N_SUB, 1, d)
    ffn_w_in = ffn_w_in.astype(BF16)
    ffn_w_out = ffn_w_out.astype(BF16)
    for l in range(depth):
        i = l // 2
        x = _ffn(x, mod, norm_pre, norm_post, (l, 0), ffn_w_in, ffn_w_out, 0)
        if l % 2 == 0:
            x = _mix0(x, mod, norm_pre, norm_post, (l, 1), ab_w_in[i], pool_w[i], pool_scale[i],
                      sgu_ln_g[i], sgu_ln_b[i], sgu_w[i], sgu_b[i], ab_w_out[i])
        else:
            slab_params = _s5_slab_params(ssm_lam_re[i], ssm_lam_im[i], ssm_b_re[i], ssm_b_im[i],
                                          ssm_c_re[i], ssm_c_im[i], ssm_d[i], ssm_log_dt[i])
            u_cm = _s5_in(x, mod, norm_pre, (l, 1), ssm_w_in[i])
            g_cm = _s5_core(u_cm, *slab_params, bsz)
            x = _s5_out(x, mod, norm_post, (l, 1), g_cm, ssm_w_glu[i])
        x = _ffn(x, mod, norm_pre, norm_post, (l, 2), ffn_w_in, ffn_w_out, 1)
    return x
```

```python
import functools

import jax
import jax.numpy as jnp
from jax import lax
from jax.experimental import pallas as pl
from jax.experimental.pallas import tpu as pltpu

F32 = jnp.float32
BF16 = jnp.bfloat16
EPS = 1e-6
N_SUB = 3
FFN_RES_WEIGHT = 0.5
POOL_WINDOWS = (2, 4, 8, 16)
SGU_HEADS = 4
SGU_CHUNK = 128
SSM_GROUP = 16
SSM_CHUNK = 16
LANES = 128
SUBLANES = 8
SLAB_GROUPS = LANES // SSM_GROUP
MXU_DIM = 256
VMEM_LIMIT_BYTES = 56 * 1024 * 1024
ROW_TILE = 512
FFN_ROW_TILE = 1024
FFN_SPLIT = 2


def _const_spec(shape, lead=()):
    nd = len(shape)
    return pl.BlockSpec((None,) * len(lead) + tuple(shape), lambda *_: tuple(lead) + (0,) * nd,
                        pipeline_mode=pl.Buffered(1))


def _params(*sem):
    return pltpu.CompilerParams(dimension_semantics=sem, vmem_limit_bytes=VMEM_LIMIT_BYTES)


def _rms(x, g):
    return x * lax.rsqrt(jnp.mean(x * x, axis=-1, keepdims=True) + EPS) * g


def _pre(x, mod, g_pre):
    return _rms(x, g_pre * (1.0 + mod[1:2])) + mod[0:1]


def _post(x, y, mod, g_post, res_weight):
    return x + _rms(y, res_weight * mod[2:3] * g_post)


def _sigmoid(v):
    return 1.0 / (1.0 + jnp.exp(-v))


def _dot(a, b):
    return jnp.dot(a, b, preferred_element_type=F32)


def _ada_kernel(c_ref, w_ref, b_ref, o_ref):
    c = c_ref[...]
    cond = (c * _sigmoid(c)).astype(BF16)
    o_ref[...] = _dot(cond, w_ref[...].astype(BF16)) + b_ref[...]


def _ada_mod(c, ada_w, ada_b):
    depth, d, n = ada_w.shape
    bsz = c.shape[0]
    tn = n // 8
    return pl.pallas_call(
        _ada_kernel,
        grid=(depth, n // tn),
        in_specs=[
            pl.BlockSpec((bsz, d), lambda l, j: (0, 0)),
            pl.BlockSpec((None, d, tn), lambda l, j: (l, 0, j)),
            pl.BlockSpec((None, 1, tn), lambda l, j: (l, 0, j)),
        ],
        out_specs=pl.BlockSpec((None, bsz, tn), lambda l, j: (l, 0, j)),
        out_shape=jax.ShapeDtypeStruct((depth, bsz, n), F32),
        compiler_params=_params("arbitrary", "arbitrary"),
    )(c, ada_w, ada_b.reshape(depth, 1, n))


def _x_spec(tm, d):
    return pl.BlockSpec((None, tm, d), lambda b, i: (b, i, 0))


def _mod_spec(sub, d):
    l, k = sub
    return pl.BlockSpec((None, None, None, 3, d), lambda b, i: (l, b, k, 0, 0))


def _norm_spec(sub, d):
    return _const_spec((1, d), lead=sub)


def _ffn_kernel(x_ref, mod_ref, gpre_ref, gpost_ref, win_ref, wout_ref, o_ref, *, d_ff, fc, n_split):
    mod = mod_ref[...]
    sub = x_ref.shape[0] // n_split
    rows = [slice(p * sub, (p + 1) * sub) for p in range(n_split)]
    h = [_pre(x_ref[r, :], mod, gpre_ref[...]).astype(BF16) for r in rows]
    acc = [jnp.zeros((sub, x_ref.shape[1]), F32)] * n_split
    for j in range(d_ff // fc):
        for p in range(n_split):
            a = _dot(h[p], win_ref[:, j * fc:(j + 1) * fc])
            b = _dot(h[p], win_ref[:, d_ff + j * fc:d_ff + (j + 1) * fc])
            act = (a * _sigmoid(a) * b).astype(BF16)
            acc[p] = acc[p] + _dot(act, wout_ref[j * fc:(j + 1) * fc, :])
    for p, r in enumerate(rows):
        o_ref[r, :] = _post(x_ref[r, :], acc[p], mod, gpost_ref[...], FFN_RES_WEIGHT)


def _ffn(x, mod, norm_pre, norm_post, sub, w_in, w_out, half):
    bsz, s, d = x.shape
    d_ff = w_out.shape[2]
    tm = min(FFN_ROW_TILE, s)
    lead = (sub[0], half)
    return pl.pallas_call(
        functools.partial(_ffn_kernel, d_ff=d_ff, fc=MXU_DIM, n_split=FFN_SPLIT),
        grid=(bsz, s // tm),
        in_specs=[_x_spec(tm, d), _mod_spec(sub, d), _norm_spec(sub, d), _norm_spec(sub, d),
                  _const_spec(w_in.shape[2:], lead), _const_spec(w_out.shape[2:], lead)],
        out_specs=_x_spec(tm, d),
        out_shape=jax.ShapeDtypeStruct(x.shape, F32),
        compiler_params=_params("arbitrary", "arbitrary"),
    )(x, mod, norm_pre, norm_post, w_in, w_out)


def _mix0_kernel(x_ref, mod_ref, gpre_ref, gpost_ref, win_ref, poolw_ref, pscale_ref, lng_ref, lnb_ref,
                 sguw_ref, sgub_ref, wout_ref, o_ref, carry_ref, *, pool_width):
    i = pl.program_id(1)
    tm = x_ref.shape[0]
    n_pool = len(POOL_WINDOWS)
    gd = pool_width // n_pool

    @pl.when(i == 0)
    def _():
        carry_ref[...] = jnp.zeros(carry_ref.shape, F32)

    x = x_ref[...]
    mod = mod_ref[...]
    h = _pre(x, mod, gpre_ref[...]).astype(BF16)
    z = _dot(h, win_ref[...])

    a = z[:, :pool_width]
    row8 = lax.broadcasted_iota(jnp.int32, (8, 1), 0)

    def shifted(cur, level, k):
        lanes = cur.shape[1]
        prev8 = carry_ref[level, :, pool_width - lanes:]
        carry_ref[level, :, pool_width - lanes:] = cur[tm - 8:, :]
        rolled = pltpu.roll(cur, k, axis=0)
        if k == 8:
            return jnp.concatenate([prev8, rolled[8:]], axis=0)
        head = jnp.where(row8 < k, pltpu.roll(prev8, k, axis=0), rolled[:8])
        return jnp.concatenate([head, rolled[8:]], axis=0)

    pos = (i * tm + 1 + lax.broadcasted_iota(jnp.int32, (tm, 1), 0)).astype(F32)
    level_sum = a
    y_a = []
    for g, w in enumerate(POOL_WINDOWS):
        level_sum = level_sum + shifted(level_sum, g, w // 2)
        mean = level_sum[:, :gd] / jnp.minimum(pos, float(w))
        dg = (mean - a[:, g * gd:(g + 1) * gd]).astype(BF16)
        y_a.append(_dot(dg, poolw_ref[g]) * pscale_ref[:, g * gd:(g + 1) * gd])
        level_sum = level_sum[:, gd:]
    y_a = jnp.concatenate(y_a, axis=-1).astype(BF16)

    zb = jax.nn.gelu(z[:, pool_width:])
    sgu_width = zb.shape[1] // 2
    hd = sgu_width // SGU_HEADS
    u = zb[:, :sgu_width]
    v = zb[:, sgu_width:]
    y_b = []
    for hh in range(SGU_HEADS):
        vh = v[:, hh * hd:(hh + 1) * hd]
        mu = jnp.mean(vh, axis=-1, keepdims=True)
        var = jnp.mean(jnp.square(vh - mu), axis=-1, keepdims=True)
        vn = (vh - mu) * lax.rsqrt(var + EPS) * lng_ref[:, hh * hd:(hh + 1) * hd] + lnb_ref[:, hh * hd:(hh + 1) * hd]
        vn = vn.astype(BF16)
        s_h = [_dot(sguw_ref[hh], vn[ck * SGU_CHUNK:(ck + 1) * SGU_CHUNK]) + sgub_ref[hh]
               for ck in range(tm // SGU_CHUNK)]
        y_b.append(u[:, hh * hd:(hh + 1) * hd] * jnp.concatenate(s_h, axis=0))
    y_b = jnp.concatenate(y_b, axis=-1).astype(BF16)

    y = _dot(y_a, wout_ref[:pool_width, :]) + _dot(y_b, wout_ref[pool_width:, :])
    o_ref[...] = _post(x, y, mod, gpost_ref[...], 1.0)


def _mix0(x, mod, norm_pre, norm_post, sub, w_in, pool_w, pool_scale, ln_g, ln_b, sgu_w, sgu_b, w_out):
    bsz, s, d = x.shape
    pool_width = pool_scale.shape[0]
    sgu_width = ln_g.shape[0]
    hd = sgu_width // SGU_HEADS
    tm = min(ROW_TILE, s)
    causal = jnp.tril(jnp.ones((SGU_CHUNK, SGU_CHUNK), dtype=bool))
    sgu_w_causal = jnp.where(causal[None], sgu_w, 0.0).astype(BF16)
    sgu_b_rows = jnp.broadcast_to(sgu_b[:, :, None], (SGU_HEADS, SGU_CHUNK, hd))
    return pl.pallas_call(
        functools.partial(_mix0_kernel, pool_width=pool_width),
        grid=(bsz, s // tm),
        in_specs=[_x_spec(tm, d), _mod_spec(sub, d), _norm_spec(sub, d), _norm_spec(sub, d),
                  _const_spec(w_in.shape), _const_spec(pool_w.shape), _const_spec((1, pool_width)),
                  _const_spec((1, sgu_width)), _const_spec((1, sgu_width)),
                  _const_spec(sgu_w.shape), _const_spec(sgu_b_rows.shape), _const_spec(w_out.shape)],
        out_specs=_x_spec(tm, d),
        out_shape=jax.ShapeDtypeStruct(x.shape, F32),
        scratch_shapes=[pltpu.VMEM((len(POOL_WINDOWS), 8, pool_width), F32)],
        compiler_params=_params("arbitrary", "arbitrary"),
    )(x, mod, norm_pre, norm_post, w_in.astype(BF16), pool_w.astype(BF16),
      pool_scale.reshape(1, pool_width), ln_g.reshape(1, sgu_width), ln_b.reshape(1, sgu_width),
      sgu_w_causal, sgu_b_rows, w_out.astype(BF16))


def _s5_in_kernel(x_ref, mod_ref, gpre_ref, w_ref, o_ref, u_ref):
    n_slab, rows, _ = o_ref.shape
    h = _pre(x_ref[...], mod_ref[...], gpre_ref[...]).astype(BF16)
    u = _dot(h, w_ref[...])
    for j in range(n_slab):
        u_ref[j] = u[:, j * LANES:(j + 1) * LANES]
    for l in range(SSM_CHUNK):
        for j in range(n_slab):
            o_ref[j, :, l * LANES:(l + 1) * LANES] = (
                u_ref[j, pl.ds(l, rows, stride=SSM_CHUNK), :].astype(BF16))


def _s5_in(x, mod, norm_pre, sub, w_in):
    bsz, s, d = x.shape
    width = w_in.shape[1]
    n_slab = width // LANES
    tm = min(ROW_TILE, s)
    rows = tm // SSM_CHUNK
    tiles = s // tm
    return pl.pallas_call(
        _s5_in_kernel,
        grid=(bsz, tiles),
        in_specs=[_x_spec(tm, d), _mod_spec(sub, d), _norm_spec(sub, d), _const_spec(w_in.shape)],
        out_specs=pl.BlockSpec((n_slab, rows, SSM_CHUNK * LANES), lambda b, i: (0, b * tiles + i, 0)),
        out_shape=jax.ShapeDtypeStruct((n_slab, bsz * s // SSM_CHUNK, SSM_CHUNK * LANES), BF16),
        scratch_shapes=[pltpu.VMEM((n_slab, tm, LANES), F32)],
        compiler_params=_params("arbitrary", "arbitrary"),
    )(x, mod, norm_pre, w_in.astype(BF16))


def _s5_core_kernel(u_ref, lam_ref, bt_ref, ct_ref, d_ref, o_ref, st_ref, p_ref, plo_ref, qt_ref, toep_ref,
                    *, bsz, n_chunk):
    n_lt = st_ref.shape[0]
    half = n_lt // 2
    n_st = half * LANES
    u = u_ref[...]

    lam_re, lam_im = lam_ref[0:1], lam_ref[1:2]
    dt = jnp.exp(lam_ref[2:3])
    tau = lax.broadcasted_iota(jnp.int32, (SSM_CHUNK + 8, 1), 0).astype(F32)
    mag = jnp.exp(lam_re * dt * tau)
    ang = lam_im * dt * tau
    ap_re, ap_im = mag * jnp.cos(ang), mag * jnp.sin(ang)
    num_re, num_im = ap_re[1:2] - 1.0, ap_im[1:2]
    den = lam_re * lam_re + lam_im * lam_im
    cf_re = (num_re * lam_re + num_im * lam_im) / den
    cf_im = (num_im * lam_re - num_re * lam_im) / den
    bt_re, bt_im = bt_ref[0], bt_ref[1]
    bb_re, bb_im = cf_re * bt_re - cf_im * bt_im, cf_re * bt_im + cf_im * bt_re
    ct_re, ct_im = ct_ref[0], ct_ref[1]

    nt = (((1,), (1,)), ((), ()))
    for l in range(SSM_CHUNK):
        blk = slice(l * LANES, (l + 1) * LANES)
        a_r, a_i = ap_re[SSM_CHUNK - 1 - l:SSM_CHUNK - l], ap_im[SSM_CHUNK - 1 - l:SSM_CHUNK - l]
        for part, cols in ((a_r * bb_re - a_i * bb_im, slice(0, n_st)),
                           (a_r * bb_im + a_i * bb_re, slice(n_st, 2 * n_st))):
            hi = part.astype(BF16)
            p_ref[blk, cols] = hi
            plo_ref[blk, cols] = (part - hi.astype(F32)).astype(BF16)
        a_r, a_i = ap_re[l + 1:l + 2], ap_im[l + 1:l + 2]
        qt_ref[blk, :n_st] = (a_r * ct_re - a_i * ct_im).astype(BF16)
        qt_ref[blk, n_st:] = (-(a_r * ct_im + a_i * ct_re)).astype(BF16)

    c_nt = jnp.concatenate([ct_re, -ct_im], axis=-1)
    c_hi = c_nt.astype(BF16)
    c_lo = (c_nt - c_hi.astype(F32)).astype(BF16)
    imp_hi = lax.dot_general(p_ref[...], jnp.concatenate([c_hi, c_lo], axis=0), nt, preferred_element_type=F32)
    imp = (imp_hi[:, :LANES] + imp_hi[:, LANES:]
           + lax.dot_general(plo_ref[...], c_hi, nt, preferred_element_type=F32)).astype(BF16)
    toep_ref[:, LANES:] = imp
    toep_ref[:(SSM_CHUNK - 1) * LANES, :LANES] = imp[LANES:]
    toep_ref[(SSM_CHUNK - 1) * LANES:, :LANES] = jnp.zeros((LANES, LANES), BF16)

    seq_rows = st_ref.shape[1] // bsz
    z = _dot(u, p_ref[...])
    for k in range(n_lt):
        for b in range(bsz):
            st_ref[k, b * seq_rows:b * seq_rows + n_chunk, :] = (
                z[b * n_chunk:(b + 1) * n_chunk, k * LANES:(k + 1) * LANES])
    a_re = [ap_re[SSM_CHUNK:SSM_CHUNK + 1, k * LANES:(k + 1) * LANES] for k in range(half)]
    a_im = [ap_im[SSM_CHUNK:SSM_CHUNK + 1, k * LANES:(k + 1) * LANES] for k in range(half)]
    s_re = [jnp.zeros((bsz, LANES), F32)] * half
    s_im = [jnp.zeros((bsz, LANES), F32)] * half
    for c in range(n_chunk):
        rows = pl.ds(c, bsz, stride=seq_rows)
        for k in range(half):
            z_re = st_ref[k, rows, :]
            z_im = st_ref[half + k, rows, :]
            st_ref[k, rows, :] = s_re[k]
            st_ref[half + k, rows, :] = s_im[k]
            s_re[k], s_im[k] = (a_re[k] * s_re[k] - a_im[k] * s_im[k] + z_re,
                                a_re[k] * s_im[k] + a_im[k] * s_re[k] + z_im)
    st = jnp.concatenate(
        [jnp.concatenate([st_ref[k, b * seq_rows:b * seq_rows + n_chunk, :] for b in range(bsz)], axis=0)
         .astype(BF16) for k in range(n_lt)], axis=-1)

    n_tile = u.shape[1] // MXU_DIM
    for n in range(n_tile):
        cols = slice(n * MXU_DIM, (n + 1) * MXU_DIM)
        y = _dot(u[:, :(n + 1) * MXU_DIM], toep_ref[(n_tile - 1 - n) * MXU_DIM:, :])
        y = y + lax.dot_general(st, qt_ref[cols, :], nt, preferred_element_type=F32)
        y = y + d_ref[:, cols] * u[:, cols].astype(F32)
        o_ref[:, cols] = jax.nn.gelu(y).astype(BF16)


def _s5_core(u_cm, lam_rows, bt, ct, d_cm, bsz):
    n_slab, rows, kdim = u_cm.shape
    n_st = lam_rows.shape[2]
    slab = lambda shape: pl.BlockSpec((None,) + shape, lambda j: (j,) + (0,) * len(shape))
    return pl.pallas_call(
        functools.partial(_s5_core_kernel, bsz=bsz, n_chunk=rows // bsz),
        grid=(n_slab,),
        in_specs=[slab((rows, kdim)), slab(lam_rows.shape[1:]), slab(bt.shape[1:]), slab(ct.shape[1:]),
                  slab(d_cm.shape[1:])],
        out_specs=slab((rows, kdim)),
        out_shape=jax.ShapeDtypeStruct(u_cm.shape, BF16),
        scratch_shapes=[pltpu.VMEM((2 * n_st // LANES, rows + SUBLANES * bsz, LANES), F32),
                        pltpu.VMEM((kdim, 2 * n_st), BF16),
                        pltpu.VMEM((kdim, 2 * n_st), BF16),
                        pltpu.VMEM((kdim, 2 * n_st), BF16),
                        pltpu.VMEM((kdim, 2 * LANES), BF16)],
        compiler_params=_params("arbitrary"),
    )(u_cm, lam_rows, bt, ct, d_cm)


def _s5_out_kernel(x_ref, mod_ref, gpost_ref, g_ref, w_ref, o_ref, g_nat_ref):
    n_slab, rows, _ = g_ref.shape
    for l in range(SSM_CHUNK):
        for j in range(n_slab):
            g_nat_ref[j, pl.ds(l, rows, stride=SSM_CHUNK), :] = (
                g_ref[j, :, l * LANES:(l + 1) * LANES].astype(F32))
    g = jnp.concatenate([g_nat_ref[j].astype(BF16) for j in range(n_slab)], axis=-1)
    ab = _dot(g, w_ref[...])
    d = ab.shape[1] // 2
    y = ab[:, :d] * _sigmoid(ab[:, d:])
    o_ref[...] = _post(x_ref[...], y, mod_ref[...], gpost_ref[...], 1.0)


def _s5_out(x, mod, norm_post, sub, g_cm, w_glu):
    bsz, s, d = x.shape
    n_slab = g_cm.shape[0]
    tm = min(ROW_TILE, s)
    rows = tm // SSM_CHUNK
    tiles = s // tm
    return pl.pallas_call(
        _s5_out_kernel,
        grid=(bsz, tiles),
        in_specs=[_x_spec(tm, d), _mod_spec(sub, d), _norm_spec(sub, d),
                  pl.BlockSpec((n_slab, rows, SSM_CHUNK * LANES), lambda b, i: (0, b * tiles + i, 0)),
                  _const_spec(w_glu.shape)],
        out_specs=_x_spec(tm, d),
        out_shape=jax.ShapeDtypeStruct(x.shape, F32),
        scratch_shapes=[pltpu.VMEM((n_slab, tm, LANES), F32)],
        compiler_params=_params("arbitrary", "arbitrary"),
    )(x, mod, norm_post, g_cm, w_glu.astype(BF16))


def _s5_slab_params(lam_re, lam_im, b_re, b_im, c_re, c_im, d_skip, log_dt):
    n_group, n_state = lam_re.shape
    n_ch = b_re.shape[2]
    n_slab = n_group // SLAB_GROUPS
    n_st = SLAB_GROUPS * n_state
    log_dt_st = jnp.broadcast_to(log_dt[:, None], (n_group, n_state))
    lam_rows = jnp.stack([v.reshape(n_slab, n_st) for v in (lam_re, lam_im, log_dt_st)], axis=1)
    same_group = jnp.eye(SLAB_GROUPS, dtype=bool)[None, :, None, :, None]

    def block_diag(w_re, w_im):
        parts = []
        for w in (w_re, w_im):
            w = w.reshape(n_slab, SLAB_GROUPS, n_ch, 1, n_state)
            parts.append(jnp.where(same_group, w, 0.0).reshape(n_slab, LANES, n_st))
        return jnp.stack(parts, axis=1)

    bt = block_diag(b_re.transpose(0, 2, 1), b_im.transpose(0, 2, 1))
    ct = block_diag(c_re, c_im)
    d_cm = jnp.tile(d_skip.reshape(n_slab, 1, LANES), (1, 1, SSM_CHUNK))
    return lam_rows, bt, ct, d_cm


def kernel(x, c, ada_w, ada_b, norm_pre, norm_post, ffn_w_in, ffn_w_out, ab_w_in, pool_w, pool_scale, sgu_ln_g, sgu_ln_b, sgu_w, sgu_b, ab_w_out, ssm_w_in, ssm_lam_re, ssm_lam_im, ssm_b_re, ssm_b_im, ssm_c_re, ssm_c_im, ssm_d, ssm_log_dt, ssm_w_glu):
    bsz, s, d = x.shape
    depth = ada_w.shape[0]
    assert s % ROW_TILE == 0 or s < ROW_TILE
    mod = _ada_mod(c, ada_w, ada_b).reshape(depth, bsz, N_SUB, 3, d)
    norm_pre = norm_pre.reshape(depth, N_SUB, 1, d)
    norm_post = norm_post.reshape(depth, N_SUB, 1, d)
    ffn_w_in = ffn_w_in.astype(BF16)
    ffn_w_out = ffn_w_out.astype(BF16)
    for l in range(depth):
        i = l // 2
        x = _ffn(x, mod, norm_pre, norm_post, (l, 0), ffn_w_in, ffn_w_out, 0)
        if l % 2 == 0:
            x = _mix0(x, mod, norm_pre, norm_post, (l, 1), ab_w_in[i], pool_w[i], pool_scale[i],
                      sgu_ln_g[i], sgu_ln_b[i], sgu_w[i], sgu_b[i], ab_w_out[i])
        else:
            slab_params = _s5_slab_params(ssm_lam_re[i], ssm_lam_im[i], ssm_b_re[i], ssm_b_im[i],
                                          ssm_c_re[i], ssm_c_im[i], ssm_d[i], ssm_log_dt[i])
            u_cm = _s5_in(x, mod, norm_pre, (l, 1), ssm_w_in[i])
            g_cm = _s5_core(u_cm, *slab_params, bsz)
            x = _s5_out(x, mod, norm_post, (l, 1), g_cm, ssm_w_glu[i])
        x = _ffn(x, mod, norm_pre, norm_post, (l, 2), ffn_w_in, ffn_w_out, 1)
    return x
```

```python
import functools

import jax
import jax.numpy as jnp
from jax import lax
from jax.experimental import pallas as pl
from jax.experimental.pallas import tpu as pltpu

F32 = jnp.float32
BF16 = jnp.bfloat16
EPS = 1e-6
N_SUB = 3
FFN_RES_WEIGHT = 0.5
POOL_WINDOWS = (2, 4, 8, 16)
SGU_HEADS = 4
SGU_CHUNK = 128
SSM_GROUP = 16
SSM_CHUNK = 16
LANES = 128
SUBLANES = 8
SLAB_GROUPS = LANES // SSM_GROUP
MXU_DIM = 256
VMEM_LIMIT_BYTES = 56 * 1024 * 1024
ROW_TILE = 512
FFN_ROW_TILE = 1024
FFN_SPLIT = 2


def _const_spec(shape, lead=()):
    nd = len(shape)
    return pl.BlockSpec((None,) * len(lead) + tuple(shape), lambda *_: tuple(lead) + (0,) * nd,
                        pipeline_mode=pl.Buffered(1))


def _params(*sem):
    return pltpu.CompilerParams(dimension_semantics=sem, vmem_limit_bytes=VMEM_LIMIT_BYTES)


def _rms(x, g):
    return x * lax.rsqrt(jnp.mean(x * x, axis=-1, keepdims=True) + EPS) * g


def _pre(x, mod, g_pre):
    return _rms(x, g_pre * (1.0 + mod[1:2])) + mod[0:1]


def _post(x, y, mod, g_post, res_weight):
    return x + _rms(y, res_weight * mod[2:3] * g_post)


def _sigmoid(v):
    return 1.0 / (1.0 + jnp.exp(-v))


def _dot(a, b):
    return jnp.dot(a, b, preferred_element_type=F32)


def _ada_kernel(c_ref, w_ref, b_ref, o_ref):
    c = c_ref[...]
    cond = (c * _sigmoid(c)).astype(BF16)
    o_ref[...] = _dot(cond, w_ref[...].astype(BF16)) + b_ref[...]


def _ada_mod(c, ada_w, ada_b):
    depth, d, n = ada_w.shape
    bsz = c.shape[0]
    tn = n // 8
    return pl.pallas_call(
        _ada_kernel,
        grid=(depth, n // tn),
        in_specs=[
            pl.BlockSpec((bsz, d), lambda l, j: (0, 0)),
            pl.BlockSpec((None, d, tn), lambda l, j: (l, 0, j)),
            pl.BlockSpec((None, 1, tn), lambda l, j: (l, 0, j)),
        ],
        out_specs=pl.BlockSpec((None, bsz, tn), lambda l, j: (l, 0, j)),
        out_shape=jax.ShapeDtypeStruct((depth, bsz, n), F32),
        compiler_params=_params("arbitrary", "arbitrary"),
    )(c, ada_w, ada_b.reshape(depth, 1, n))


def _x_spec(tm, d):
    return pl.BlockSpec((None, tm, d), lambda b, i: (b, i, 0))


def _mod_spec(sub, d):
    l, k = sub
    return pl.BlockSpec((None, None, None, 3, d), lambda b, i: (l, b, k, 0, 0))


def _norm_spec(sub, d):
    return _const_spec((1, d), lead=sub)


def _ffn_kernel(x_ref, mod_ref, gpre_ref, gpost_ref, win_ref, wout_ref, o_ref, *, d_ff, fc, n_split):
    mod = mod_ref[...]
    sub = x_ref.shape[0] // n_split
    rows = [slice(p * sub, (p + 1) * sub) for p in range(n_split)]
    h = [_pre(x_ref[r, :], mod, gpre_ref[...]).astype(BF16) for r in rows]
    acc = [jnp.zeros((sub, x_ref.shape[1]), F32)] * n_split
    for j in range(d_ff // fc):
        for p in range(n_split):
            a = _dot(h[p], win_ref[:, j * fc:(j + 1) * fc])
            b = _dot(h[p], win_ref[:, d_ff + j * fc:d_ff + (j + 1) * fc])
            act = (a * _sigmoid(a) * b).astype(BF16)
            acc[p] = acc[p] + _dot(act, wout_ref[j * fc:(j + 1) * fc, :])
    for p, r in enumerate(rows):
        o_ref[r, :] = _post(x_ref[r, :], acc[p], mod, gpost_ref[...], FFN_RES_WEIGHT)


def _ffn(x, mod, norm_pre, norm_post, sub, w_in, w_out, half):
    bsz, s, d = x.shape
    d_ff = w_out.shape[2]
    tm = min(FFN_ROW_TILE, s)
    lead = (sub[0], half)
    return pl.pallas_call(
        functools.partial(_ffn_kernel, d_ff=d_ff, fc=MXU_DIM, n_split=FFN_SPLIT),
        grid=(bsz, s // tm),
        in_specs=[_x_spec(tm, d), _mod_spec(sub, d), _norm_spec(sub, d), _norm_spec(sub, d),
                  _const_spec(w_in.shape[2:], lead), _const_spec(w_out.shape[2:], lead)],
        out_specs=_x_spec(tm, d),
        out_shape=jax.ShapeDtypeStruct(x.shape, F32),
        compiler_params=_params("arbitrary", "arbitrary"),
    )(x, mod, norm_pre, norm_post, w_in, w_out)


def _mix0_kernel(x_ref, mod_ref, gpre_ref, gpost_ref, win_ref, poolw_ref, pscale_ref, lng_ref, lnb_ref,
                 sguw_ref, sgub_ref, wout_ref, o_ref, carry_ref, *, pool_width):
    i = pl.program_id(1)
    tm = x_ref.shape[0]
    n_pool = len(POOL_WINDOWS)
    gd = pool_width // n_pool

    @pl.when(i == 0)
    def _():
        carry_ref[...] = jnp.zeros(carry_ref.shape, F32)

    x = x_ref[...]
    mod = mod_ref[...]
    h = _pre(x, mod, gpre_ref[...]).astype(BF16)
    z = _dot(h, win_ref[...])

    a = z[:, :pool_width]
    row8 = lax.broadcasted_iota(jnp.int32, (8, 1), 0)

    def shifted(cur, level, k):
        lanes = cur.shape[1]
        prev8 = carry_ref[level, :, pool_width - lanes:]
        carry_ref[level, :, pool_width - lanes:] = cur[tm - 8:, :]
        rolled = pltpu.roll(cur, k, axis=0)
        if k == 8:
            return jnp.concatenate([prev8, rolled[8:]], axis=0)
        head = jnp.where(row8 < k, pltpu.roll(prev8, k, axis=0), rolled[:8])
        return jnp.concatenate([head, rolled[8:]], axis=0)

    pos = (i * tm + 1 + lax.broadcasted_iota(jnp.int32, (tm, 1), 0)).astype(F32)
    level_sum = a
    y_a = []
    for g, w in enumerate(POOL_WINDOWS):
        level_sum = level_sum + shifted(level_sum, g, w // 2)
        mean = level_sum[:, :gd] / jnp.minimum(pos, float(w))
        dg = (mean - a[:, g * gd:(g + 1) * gd]).astype(BF16)
        y_a.append(_dot(dg, poolw_ref[g]) * pscale_ref[:, g * gd:(g + 1) * gd])
        level_sum = level_sum[:, gd:]
    y_a = jnp.concatenate(y_a, axis=-1).astype(BF16)

    zb = jax.nn.gelu(z[:, pool_width:])
    sgu_width = zb.shape[1] // 2
    hd = sgu_width // SGU_HEADS
    u = zb[:, :sgu_width]
    v = zb[:, sgu_width:]
    y_b = []
    for hh in range(SGU_HEADS):
        vh = v[:, hh * hd:(hh + 1) * hd]
        mu = jnp.mean(vh, axis=-1, keepdims=True)
        var = jnp.mean(jnp.square(vh - mu), axis=-1, keepdims=True)
        vn = (vh - mu) * lax.rsqrt(var + EPS) * lng_ref[:, hh * hd:(hh + 1) * hd] + lnb_ref[:, hh * hd:(hh + 1) * hd]
        vn = vn.astype(BF16)
        s_h = [_dot(sguw_ref[hh], vn[ck * SGU_CHUNK:(ck + 1) * SGU_CHUNK]) + sgub_ref[hh]
               for ck in range(tm // SGU_CHUNK)]
        y_b.append(u[:, hh * hd:(hh + 1) * hd] * jnp.concatenate(s_h, axis=0))
    y_b = jnp.concatenate(y_b, axis=-1).astype(BF16)

    y = _dot(y_a, wout_ref[:pool_width, :]) + _dot(y_b, wout_ref[pool_width:, :])
    o_ref[...] = _post(x, y, mod, gpost_ref[...], 1.0)


def _mix0(x, mod, norm_pre, norm_post, sub, w_in, pool_w, pool_scale, ln_g, ln_b, sgu_w, sgu_b, w_out):
    bsz, s, d = x.shape
    pool_width = pool_scale.shape[0]
    sgu_width = ln_g.shape[0]
    hd = sgu_width // SGU_HEADS
    tm = min(ROW_TILE, s)
    causal = jnp.tril(jnp.ones((SGU_CHUNK, SGU_CHUNK), dtype=bool))
    sgu_w_causal = jnp.where(causal[None], sgu_w, 0.0).astype(BF16)
    sgu_b_rows = jnp.broadcast_to(sgu_b[:, :, None], (SGU_HEADS, SGU_CHUNK, hd))
    return pl.pallas_call(
        functools.partial(_mix0_kernel, pool_width=pool_width),
        grid=(bsz, s // tm),
        in_specs=[_x_spec(tm, d), _mod_spec(sub, d), _norm_spec(sub, d), _norm_spec(sub, d),
                  _const_spec(w_in.shape), _const_spec(pool_w.shape), _const_spec((1, pool_width)),
                  _const_spec((1, sgu_width)), _const_spec((1, sgu_width)),
                  _const_spec(sgu_w.shape), _const_spec(sgu_b_rows.shape), _const_spec(w_out.shape)],
        out_specs=_x_spec(tm, d),
        out_shape=jax.ShapeDtypeStruct(x.shape, F32),
        scratch_shapes=[pltpu.VMEM((len(POOL_WINDOWS), 8, pool_width), F32)],
        compiler_params=_params("arbitrary", "arbitrary"),
    )(x, mod, norm_pre, norm_post, w_in.astype(BF16), pool_w.astype(BF16),
      pool_scale.reshape(1, pool_width), ln_g.reshape(1, sgu_width), ln_b.reshape(1, sgu_width),
      sgu_w_causal, sgu_b_rows, w_out.astype(BF16))


def _chunk_rows(x_ref, d):
    return jnp.concatenate([x_ref[:, l * d:(l + 1) * d] for l in range(SSM_CHUNK)], axis=0)


def _chunk_spec(tm, d):
    return pl.BlockSpec((None, tm // SSM_CHUNK, SSM_CHUNK * d), lambda b, i: (b, i, 0))


def _s5_in_kernel(x_ref, mod_ref, gpre_ref, w_ref, o_ref):
    n_slab, rows, _ = o_ref.shape
    d = w_ref.shape[0]
    h = _pre(_chunk_rows(x_ref, d), mod_ref[...], gpre_ref[...]).astype(BF16)
    u = _dot(h, w_ref[...]).astype(BF16)
    for l in range(SSM_CHUNK):
        for j in range(n_slab):
            o_ref[j, :, l * LANES:(l + 1) * LANES] = u[l * rows:(l + 1) * rows, j * LANES:(j + 1) * LANES]


def _s5_in(x, mod, norm_pre, sub, w_in):
    bsz, s, d = x.shape
    width = w_in.shape[1]
    n_slab = width // LANES
    tm = min(ROW_TILE, s)
    rows = tm // SSM_CHUNK
    tiles = s // tm
    return pl.pallas_call(
        _s5_in_kernel,
        grid=(bsz, tiles),
        in_specs=[_chunk_spec(tm, d), _mod_spec(sub, d), _norm_spec(sub, d), _const_spec(w_in.shape)],
        out_specs=pl.BlockSpec((n_slab, rows, SSM_CHUNK * LANES), lambda b, i: (0, b * tiles + i, 0)),
        out_shape=jax.ShapeDtypeStruct((n_slab, bsz * s // SSM_CHUNK, SSM_CHUNK * LANES), BF16),
        compiler_params=_params("arbitrary", "arbitrary"),
    )(x.reshape(bsz, s // SSM_CHUNK, SSM_CHUNK * d), mod, norm_pre, w_in.astype(BF16))


def _s5_core_kernel(u_ref, lam_ref, bt_ref, ct_ref, d_ref, o_ref, st_ref, p_ref, plo_ref, qt_ref, toep_ref,
                    *, bsz, n_chunk):
    n_lt = st_ref.shape[0]
    half = n_lt // 2
    n_st = half * LANES
    u = u_ref[...]

    lam_re, lam_im = lam_ref[0:1], lam_ref[1:2]
    dt = jnp.exp(lam_ref[2:3])
    tau = lax.broadcasted_iota(jnp.int32, (SSM_CHUNK + 8, 1), 0).astype(F32)
    mag = jnp.exp(lam_re * dt * tau)
    ang = lam_im * dt * tau
    ap_re, ap_im = mag * jnp.cos(ang), mag * jnp.sin(ang)
    num_re, num_im = ap_re[1:2] - 1.0, ap_im[1:2]
    den = lam_re * lam_re + lam_im * lam_im
    cf_re = (num_re * lam_re + num_im * lam_im) / den
    cf_im = (num_im * lam_re - num_re * lam_im) / den
    bt_re, bt_im = bt_ref[0], bt_ref[1]
    bb_re, bb_im = cf_re * bt_re - cf_im * bt_im, cf_re * bt_im + cf_im * bt_re
    ct_re, ct_im = ct_ref[0], ct_ref[1]

    nt = (((1,), (1,)), ((), ()))
    for l in range(SSM_CHUNK):
        blk = slice(l * LANES, (l + 1) * LANES)
        a_r, a_i = ap_re[SSM_CHUNK - 1 - l:SSM_CHUNK - l], ap_im[SSM_CHUNK - 1 - l:SSM_CHUNK - l]
        for part, cols in ((a_r * bb_re - a_i * bb_im, slice(0, n_st)),
                           (a_r * bb_im + a_i * bb_re, slice(n_st, 2 * n_st))):
            hi = part.astype(BF16)
            p_ref[blk, cols] = hi
            plo_ref[blk, cols] = (part - hi.astype(F32)).astype(BF16)
        a_r, a_i = ap_re[l + 1:l + 2], ap_im[l + 1:l + 2]
        qt_ref[blk, :n_st] = (a_r * ct_re - a_i * ct_im).astype(BF16)
        qt_ref[blk, n_st:] = (-(a_r * ct_im + a_i * ct_re)).astype(BF16)

    c_nt = jnp.concatenate([ct_re, -ct_im], axis=-1)
    c_hi = c_nt.astype(BF16)
    c_lo = (c_nt - c_hi.astype(F32)).astype(BF16)
    imp_hi = lax.dot_general(p_ref[...], jnp.concatenate([c_hi, c_lo], axis=0), nt, preferred_element_type=F32)
    imp = (imp_hi[:, :LANES] + imp_hi[:, LANES:]
           + lax.dot_general(plo_ref[...], c_hi, nt, preferred_element_type=F32)).astype(BF16)
    toep_ref[:, LANES:] = imp
    toep_ref[:(SSM_CHUNK - 1) * LANES, :LANES] = imp[LANES:]
    toep_ref[(SSM_CHUNK - 1) * LANES:, :LANES] = jnp.zeros((LANES, LANES), BF16)

    seq_rows = st_ref.shape[1] // bsz
    z = _dot(u, p_ref[...])
    for k in range(n_lt):
        for b in range(bsz):
            st_ref[k, b * seq_rows:b * seq_rows + n_chunk, :] = (
                z[b * n_chunk:(b + 1) * n_chunk, k * LANES:(k + 1) * LANES])
    a_re = [ap_re[SSM_CHUNK:SSM_CHUNK + 1, k * LANES:(k + 1) * LANES] for k in range(half)]
    a_im = [ap_im[SSM_CHUNK:SSM_CHUNK + 1, k * LANES:(k + 1) * LANES] for k in range(half)]
    s_re = [jnp.zeros((bsz, LANES), F32)] * half
    s_im = [jnp.zeros((bsz, LANES), F32)] * half
    for c in range(n_chunk):
        rows = pl.ds(c, bsz, stride=seq_rows)
        for k in range(half):
            z_re = st_ref[k, rows, :]
            z_im = st_ref[half + k, rows, :]
            st_ref[k, rows, :] = s_re[k]
            st_ref[half + k, rows, :] = s_im[k]
            s_re[k], s_im[k] = (a_re[k] * s_re[k] - a_im[k] * s_im[k] + z_re,
                                a_re[k] * s_im[k] + a_im[k] * s_re[k] + z_im)
    st = jnp.concatenate(
        [jnp.concatenate([st_ref[k, b * seq_rows:b * seq_rows + n_chunk, :] for b in range(bsz)], axis=0)
         .astype(BF16) for k in range(n_lt)], axis=-1)

    def finish(cols, y):
        y = y + d_ref[:, cols] * u[:, cols].astype(F32)
        o_ref[:, cols] = jax.nn.gelu(y).astype(BF16)

    n_tile = u.shape[1] // MXU_DIM
    pending = None
    for n in range(n_tile):
        cols = slice(n * MXU_DIM, (n + 1) * MXU_DIM)
        y = _dot(u[:, :(n + 1) * MXU_DIM], toep_ref[(n_tile - 1 - n) * MXU_DIM:, :])
        y = y + lax.dot_general(st, qt_ref[cols, :], nt, preferred_element_type=F32)
        if pending is not None:
            finish(*pending)
        pending = (cols, y)
    finish(*pending)


def _s5_core(u_cm, lam_rows, bt, ct, d_cm, bsz):
    n_slab, rows, kdim = u_cm.shape
    n_st = lam_rows.shape[2]
    slab = lambda shape: pl.BlockSpec((None,) + shape, lambda j: (j,) + (0,) * len(shape))
    return pl.pallas_call(
        functools.partial(_s5_core_kernel, bsz=bsz, n_chunk=rows // bsz),
        grid=(n_slab,),
        in_specs=[slab((rows, kdim)), slab(lam_rows.shape[1:]), slab(bt.shape[1:]), slab(ct.shape[1:]),
                  slab(d_cm.shape[1:])],
        out_specs=slab((rows, kdim)),
        out_shape=jax.ShapeDtypeStruct(u_cm.shape, BF16),
        scratch_shapes=[pltpu.VMEM((2 * n_st // LANES, rows + SUBLANES * bsz, LANES), F32),
                        pltpu.VMEM((kdim, 2 * n_st), BF16),
                        pltpu.VMEM((kdim, 2 * n_st), BF16),
                        pltpu.VMEM((kdim, 2 * n_st), BF16),
                        pltpu.VMEM((kdim, 2 * LANES), BF16)],
        compiler_params=_params("arbitrary"),
    )(u_cm, lam_rows, bt, ct, d_cm)


def _s5_out_kernel(x_ref, mod_ref, gpost_ref, g_ref, w_ref, o_ref):
    n_slab, rows, _ = g_ref.shape
    d = w_ref.shape[1] // 2
    g = jnp.concatenate(
        [jnp.concatenate([g_ref[j, :, l * LANES:(l + 1) * LANES] for j in range(n_slab)], axis=-1)
         for l in range(SSM_CHUNK)], axis=0)
    ab = _dot(g, w_ref[...])
    y = ab[:, :d] * _sigmoid(ab[:, d:])
    out = _post(_chunk_rows(x_ref, d), y, mod_ref[...], gpost_ref[...], 1.0)
    for l in range(SSM_CHUNK):
        o_ref[:, l * d:(l + 1) * d] = out[l * rows:(l + 1) * rows, :]


def _s5_out(x, mod, norm_post, sub, g_cm, w_glu):
    bsz, s, d = x.shape
    n_slab = g_cm.shape[0]
    tm = min(ROW_TILE, s)
    rows = tm // SSM_CHUNK
    tiles = s // tm
    out = pl.pallas_call(
        _s5_out_kernel,
        grid=(bsz, tiles),
        in_specs=[_chunk_spec(tm, d), _mod_spec(sub, d), _norm_spec(sub, d),
                  pl.BlockSpec((n_slab, rows, SSM_CHUNK * LANES), lambda b, i: (0, b * tiles + i, 0)),
                  _const_spec(w_glu.shape)],
        out_specs=_chunk_spec(tm, d),
        out_shape=jax.ShapeDtypeStruct((bsz, s // SSM_CHUNK, SSM_CHUNK * d), F32),
        compiler_params=_params("arbitrary", "arbitrary"),
    )(x.reshape(bsz, s // SSM_CHUNK, SSM_CHUNK * d), mod, norm_post, g_cm, w_glu.astype(BF16))
    return out.reshape(bsz, s, d)


def _s5_slab_params(lam_re, lam_im, b_re, b_im, c_re, c_im, d_skip, log_dt):
    n_group, n_state = lam_re.shape
    n_ch = b_re.shape[2]
    n_slab = n_group // SLAB_GROUPS
    n_st = SLAB_GROUPS * n_state
    log_dt_st = jnp.broadcast_to(log_dt[:, None], (n_group, n_state))
    lam_rows = jnp.stack([v.reshape(n_slab, n_st) for v in (lam_re, lam_im, log_dt_st)], axis=1)
    same_group = jnp.eye(SLAB_GROUPS, dtype=bool)[None, :, None, :, None]

    def block_diag(w_re, w_im):
        parts = []
        for w in (w_re, w_im):
            w = w.reshape(n_slab, SLAB_GROUPS, n_ch, 1, n_state)
            parts.append(jnp.where(same_group, w, 0.0).reshape(n_slab, LANES, n_st))
        return jnp.stack(parts, axis=1)

    bt = block_diag(b_re.transpose(0, 2, 1), b_im.transpose(0, 2, 1))
    ct = block_diag(c_re, c_im)
    d_cm = jnp.tile(d_skip.reshape(n_slab, 1, LANES), (1, 1, SSM_CHUNK))
    return lam_rows, bt, ct, d_cm


def kernel(x, c, ada_w, ada_b, norm_pre, norm_post, ffn_w_in, ffn_w_out, ab_w_in, pool_w, pool_scale, sgu_ln_g, sgu_ln_b, sgu_w, sgu_b, ab_w_out, ssm_w_in, ssm_lam_re, ssm_lam_im, ssm_b_re, ssm_b_im, ssm_c_re, ssm_c_im, ssm_d, ssm_log_dt, ssm_w_glu):
    bsz, s, d = x.shape
    depth = ada_w.shape[0]
    assert s % ROW_TILE == 0 or s < ROW_TILE
    mod = _ada_mod(c, ada_w, ada_b).reshape(depth, bsz, N_SUB, 3, d)
    norm_pre = norm_pre.reshape(depth, N_SUB, 1, d)
    norm_post = norm_post.reshape(depth, N_SUB, 1, d)
    ffn_w_in = ffn_w_in.astype(BF16)
    ffn_w_out = ffn_w_out.astype(BF16)
    for l in range(depth):
        i = l // 2
        x = _ffn(x, mod, norm_pre, norm_post, (l, 0), ffn_w_in, ffn_w_out, 0)
        if l % 2 == 0:
            x = _mix0(x, mod, norm_pre, norm_post, (l, 1), ab_w_in[i], pool_w[i], pool_scale[i],
                      sgu_ln_g[i], sgu_ln_b[i], sgu_w[i], sgu_b[i], ab_w_out[i])
        else:
            slab_params = _s5_slab_params(ssm_lam_re[i], ssm_lam_im[i], ssm_b_re[i], ssm_b_im[i],
                                          ssm_c_re[i], ssm_c_im[i], ssm_d[i], ssm_log_dt[i])
            u_cm = _s5_in(x, mod, norm_pre, (l, 1), ssm_w_in[i])
            g_cm = _s5_core(u_cm, *slab_params, bsz)
            x = _s5_out(x, mod, norm_post, (l, 1), g_cm, ssm_w_glu[i])
        x = _ffn(x, mod, norm_pre, norm_post, (l, 2), ffn_w_in, ffn_w_out, 1)
    return x
```

```python
import functools

import jax
import jax.numpy as jnp
from jax import lax
from jax.experimental import pallas as pl
from jax.experimental.pallas import tpu as pltpu

F32 = jnp.float32
BF16 = jnp.bfloat16
EPS = 1e-6
N_SUB = 3
FFN_RES_WEIGHT = 0.5
POOL_WINDOWS = (2, 4, 8, 16)
SGU_HEADS = 4
SGU_CHUNK = 128
SSM_GROUP = 16
SSM_CHUNK = 16
LANES = 128
SUBLANES = 8
SLAB_GROUPS = LANES // SSM_GROUP
MXU_DIM = 256
VMEM_LIMIT_BYTES = 56 * 1024 * 1024
ROW_TILE = 512
FFN_ROW_TILE = 1024
FFN_SPLIT = 2


def _const_spec(shape, lead=()):
    nd = len(shape)
    return pl.BlockSpec((None,) * len(lead) + tuple(shape), lambda *_: tuple(lead) + (0,) * nd,
                        pipeline_mode=pl.Buffered(1))


def _params(*sem):
    return pltpu.CompilerParams(dimension_semantics=sem, vmem_limit_bytes=VMEM_LIMIT_BYTES)


def _rms(x, g):
    return x * lax.rsqrt(jnp.mean(x * x, axis=-1, keepdims=True) + EPS) * g


def _mod_rows(mod_ref):
    b = pl.program_id(0)
    return [mod_ref[r, pl.ds(b, 1), :] for r in range(3)]


def _pre(x, mod, g_pre):
    return _rms(x, g_pre * (1.0 + mod[1])) + mod[0]


def _post(x, y, mod, g_post, res_weight):
    return x + _rms(y, res_weight * mod[2] * g_post)


def _sigmoid(v):
    return 1.0 / (1.0 + jnp.exp(-v))


def _dot(a, b):
    return jnp.dot(a, b, preferred_element_type=F32)


def _ada_kernel(c_ref, w_ref, b_ref, o_ref):
    c = c_ref[...]
    cond = (c * _sigmoid(c)).astype(BF16)
    o_ref[...] = _dot(cond, w_ref[...].astype(BF16)) + b_ref[...]


def _ada_mod(c, ada_w, ada_b):
    depth, d, n = ada_w.shape
    bsz = c.shape[0]
    out = pl.pallas_call(
        _ada_kernel,
        grid=(depth, n // d),
        in_specs=[
            pl.BlockSpec((bsz, d), lambda l, j: (0, 0)),
            pl.BlockSpec((None, d, d), lambda l, j: (l, 0, j)),
            pl.BlockSpec((None, 1, d), lambda l, j: (l, 0, j)),
        ],
        out_specs=pl.BlockSpec((None, None, bsz, d), lambda l, j: (l, j, 0, 0)),
        out_shape=jax.ShapeDtypeStruct((depth, n // d, bsz, d), F32),
        compiler_params=_params("arbitrary", "arbitrary"),
    )(c, ada_w, ada_b.reshape(depth, 1, n))
    return out.reshape(depth, N_SUB, 3, bsz, d)


def _x_spec(tm, d):
    return pl.BlockSpec((None, tm, d), lambda b, i: (b, i, 0))


def _mod_spec(sub, mod):
    return _const_spec(mod.shape[2:], lead=sub)


def _norm_spec(sub, d):
    return _const_spec((N_SUB, d), lead=sub[:1])


def _ffn_kernel(x_ref, mod_ref, gpre_ref, gpost_ref, win_ref, wout_ref, *rest, k, d_ff, fc, n_split):
    n_cast = len(rest) // 2
    o_ref = rest[n_cast]
    for src_ref, dst_ref in zip(rest[:n_cast], rest[n_cast + 1:]):
        dst_ref[...] = src_ref[...].astype(BF16)

    mod = _mod_rows(mod_ref)
    sub = x_ref.shape[0] // n_split
    rows = [slice(p * sub, (p + 1) * sub) for p in range(n_split)]
    h = [_pre(x_ref[r, :], mod, gpre_ref[k:k + 1, :]).astype(BF16) for r in rows]
    acc = [jnp.zeros((sub, x_ref.shape[1]), F32)] * n_split
    for j in range(d_ff // fc):
        for p in range(n_split):
            a = _dot(h[p], win_ref[:, j * fc:(j + 1) * fc])
            b = _dot(h[p], win_ref[:, d_ff + j * fc:d_ff + (j + 1) * fc])
            act = (a * _sigmoid(a) * b).astype(BF16)
            acc[p] = acc[p] + _dot(act, wout_ref[j * fc:(j + 1) * fc, :])
    for p, r in enumerate(rows):
        o_ref[r, :] = _post(x_ref[r, :], acc[p], mod, gpost_ref[k:k + 1, :], FFN_RES_WEIGHT)


def _ffn(x, mod, norm_pre, norm_post, sub, w_in, w_out, casts=()):
    bsz, s, d = x.shape
    d_ff = w_out.shape[0]
    tm = min(FFN_ROW_TILE, s)
    tiles = s // tm
    steps = bsz * tiles
    cast_in, cast_out, cast_shapes = [], [], []
    for w, lead in casts:
        r, c = w.shape[len(lead):]
        cast_in.append(pl.BlockSpec((None,) * len(lead) + (r // steps, c),
                                    lambda b, i, lead=lead: lead + (b * tiles + i, 0)))
        cast_out.append(pl.BlockSpec((r // steps, c), lambda b, i: (b * tiles + i, 0)))
        cast_shapes.append(jax.ShapeDtypeStruct((r, c), BF16))
    out = pl.pallas_call(
        functools.partial(_ffn_kernel, k=sub[1], d_ff=d_ff, fc=MXU_DIM, n_split=FFN_SPLIT),
        grid=(bsz, tiles),
        in_specs=[_x_spec(tm, d), _mod_spec(sub, mod), _norm_spec(sub, d), _norm_spec(sub, d),
                  _const_spec(w_in.shape), _const_spec(w_out.shape)] + cast_in,
        out_specs=[_x_spec(tm, d)] + cast_out,
        out_shape=[jax.ShapeDtypeStruct(x.shape, F32)] + cast_shapes,
        compiler_params=_params("arbitrary", "arbitrary"),
    )(x, mod, norm_pre, norm_post, w_in, w_out, *[w for w, _ in casts])
    return out[0], list(out[1:])


def _mix0_kernel(x_ref, mod_ref, gpre_ref, gpost_ref, win_ref, poolw_ref, pscale_ref, lng_ref, lnb_ref,
                 sguw_ref, sgub_ref, wout_ref, o_ref, carry_ref, *, k, pool_width):
    i = pl.program_id(1)
    tm = x_ref.shape[0]
    n_pool = len(POOL_WINDOWS)
    gd = pool_width // n_pool

    @pl.when(i == 0)
    def _():
        carry_ref[...] = jnp.zeros(carry_ref.shape, F32)

    x = x_ref[...]
    mod = _mod_rows(mod_ref)
    h = _pre(x, mod, gpre_ref[k:k + 1, :]).astype(BF16)
    z = _dot(h, win_ref[...])

    a = z[:, :pool_width]
    row8 = lax.broadcasted_iota(jnp.int32, (8, 1), 0)

    def shifted(cur, level, k):
        lanes = cur.shape[1]
        prev8 = carry_ref[level, :, pool_width - lanes:]
        carry_ref[level, :, pool_width - lanes:] = cur[tm - 8:, :]
        rolled = pltpu.roll(cur, k, axis=0)
        if k == 8:
            return jnp.concatenate([prev8, rolled[8:]], axis=0)
        head = jnp.where(row8 < k, pltpu.roll(prev8, k, axis=0), rolled[:8])
        return jnp.concatenate([head, rolled[8:]], axis=0)

    pos = (i * tm + 1 + lax.broadcasted_iota(jnp.int32, (tm, 1), 0)).astype(F32)
    level_sum = a
    y_a = []
    for g, w in enumerate(POOL_WINDOWS):
        level_sum = level_sum + shifted(level_sum, g, w // 2)
        mean = level_sum[:, :gd] / jnp.minimum(pos, float(w))
        dg = (mean - a[:, g * gd:(g + 1) * gd]).astype(BF16)
        y_a.append(_dot(dg, poolw_ref[g]) * pscale_ref[:, g * gd:(g + 1) * gd])
        level_sum = level_sum[:, gd:]
    y_a = jnp.concatenate(y_a, axis=-1).astype(BF16)

    zb = jax.nn.gelu(z[:, pool_width:])
    sgu_width = zb.shape[1] // 2
    hd = sgu_width // SGU_HEADS
    u = zb[:, :sgu_width]
    v = zb[:, sgu_width:]
    y_b = []
    for hh in range(SGU_HEADS):
        vh = v[:, hh * hd:(hh + 1) * hd]
        mu = jnp.mean(vh, axis=-1, keepdims=True)
        var = jnp.mean(jnp.square(vh - mu), axis=-1, keepdims=True)
        vn = (vh - mu) * lax.rsqrt(var + EPS) * lng_ref[:, hh * hd:(hh + 1) * hd] + lnb_ref[:, hh * hd:(hh + 1) * hd]
        vn = vn.astype(BF16)
        s_h = [_dot(sguw_ref[hh], vn[ck * SGU_CHUNK:(ck + 1) * SGU_CHUNK]) + sgub_ref[hh]
               for ck in range(tm // SGU_CHUNK)]
        y_b.append(u[:, hh * hd:(hh + 1) * hd] * jnp.concatenate(s_h, axis=0))
    y_b = jnp.concatenate(y_b, axis=-1).astype(BF16)

    y = _dot(y_a, wout_ref[:pool_width, :]) + _dot(y_b, wout_ref[pool_width:, :])
    o_ref[...] = _post(x, y, mod, gpost_ref[k:k + 1, :], 1.0)


def _mix0(x, mod, norm_pre, norm_post, sub, w_in, pool_w, pool_scale, ln_g, ln_b, sgu_w, sgu_b, w_out):
    bsz, s, d = x.shape
    pool_width = pool_scale.shape[0]
    sgu_width = ln_g.shape[0]
    hd = sgu_width // SGU_HEADS
    tm = min(ROW_TILE, s)
    causal = jnp.tril(jnp.ones((SGU_CHUNK, SGU_CHUNK), dtype=bool))
    sgu_w_causal = jnp.where(causal[None], sgu_w, 0.0).astype(BF16)
    sgu_b_rows = jnp.broadcast_to(sgu_b[:, :, None], (SGU_HEADS, SGU_CHUNK, hd))
    return pl.pallas_call(
        functools.partial(_mix0_kernel, k=sub[1], pool_width=pool_width),
        grid=(bsz, s // tm),
        in_specs=[_x_spec(tm, d), _mod_spec(sub, mod), _norm_spec(sub, d), _norm_spec(sub, d),
                  _const_spec(w_in.shape), _const_spec(pool_w.shape), _const_spec((1, pool_width)),
                  _const_spec((1, sgu_width)), _const_spec((1, sgu_width)),
                  _const_spec(sgu_w.shape), _const_spec(sgu_b_rows.shape), _const_spec(w_out.shape)],
        out_specs=_x_spec(tm, d),
        out_shape=jax.ShapeDtypeStruct(x.shape, F32),
        scratch_shapes=[pltpu.VMEM((len(POOL_WINDOWS), 8, pool_width), F32)],
        compiler_params=_params("arbitrary", "arbitrary"),
    )(x, mod, norm_pre, norm_post, w_in, pool_w.astype(BF16),
      pool_scale.reshape(1, pool_width), ln_g.reshape(1, sgu_width), ln_b.reshape(1, sgu_width),
      sgu_w_causal, sgu_b_rows, w_out)


def _s5_in_kernel(x_ref, mod_ref, gpre_ref, w_ref, o_ref, u_ref, *, k):
    n_slab, rows, _ = o_ref.shape
    h = _pre(x_ref[...], _mod_rows(mod_ref), gpre_ref[k:k + 1, :]).astype(BF16)
    u = _dot(h, w_ref[...])
    for j in range(n_slab):
        u_ref[j] = u[:, j * LANES:(j + 1) * LANES]
    for l in range(SSM_CHUNK):
        for j in range(n_slab):
            o_ref[j, :, l * LANES:(l + 1) * LANES] = (
                u_ref[j, pl.ds(l, rows, stride=SSM_CHUNK), :].astype(BF16))


def _s5_in(x, mod, norm_pre, sub, w_in):
    bsz, s, d = x.shape
    width = w_in.shape[1]
    n_slab = width // LANES
    tm = min(ROW_TILE, s)
    rows = tm // SSM_CHUNK
    tiles = s // tm
    return pl.pallas_call(
        functools.partial(_s5_in_kernel, k=sub[1]),
        grid=(bsz, tiles),
        in_specs=[_x_spec(tm, d), _mod_spec(sub, mod), _norm_spec(sub, d), _const_spec(w_in.shape)],
        out_specs=pl.BlockSpec((n_slab, rows, SSM_CHUNK * LANES), lambda b, i: (0, b * tiles + i, 0)),
        out_shape=jax.ShapeDtypeStruct((n_slab, bsz * s // SSM_CHUNK, SSM_CHUNK * LANES), BF16),
        scratch_shapes=[pltpu.VMEM((n_slab, tm, LANES), F32)],
        compiler_params=_params("arbitrary", "arbitrary"),
    )(x, mod, norm_pre, w_in)


def _s5_core_kernel(u_ref, lam_ref, bt_ref, ct_ref, d_ref, o_ref, st_ref, p_ref, plo_ref, qt_ref, toep_ref,
                    *, bsz, n_chunk):
    n_lt = st_ref.shape[0]
    half = n_lt // 2
    n_st = half * LANES
    u = u_ref[...]

    lam_re, lam_im = lam_ref[0:1], lam_ref[1:2]
    dt = jnp.exp(lam_ref[2:3])
    tau = lax.broadcasted_iota(jnp.int32, (SSM_CHUNK + 8, 1), 0).astype(F32)
    mag = jnp.exp(lam_re * dt * tau)
    ang = lam_im * dt * tau
    ap_re, ap_im = mag * jnp.cos(ang), mag * jnp.sin(ang)
    num_re, num_im = ap_re[1:2] - 1.0, ap_im[1:2]
    den = lam_re * lam_re + lam_im * lam_im
    cf_re = (num_re * lam_re + num_im * lam_im) / den
    cf_im = (num_im * lam_re - num_re * lam_im) / den
    bt_re, bt_im = bt_ref[0], bt_ref[1]
    bb_re, bb_im = cf_re * bt_re - cf_im * bt_im, cf_re * bt_im + cf_im * bt_re
    ct_re, ct_im = ct_ref[0], ct_ref[1]

    nt = (((1,), (1,)), ((), ()))
    for l in range(SSM_CHUNK):
        blk = slice(l * LANES, (l + 1) * LANES)
        a_r, a_i = ap_re[SSM_CHUNK - 1 - l:SSM_CHUNK - l], ap_im[SSM_CHUNK - 1 - l:SSM_CHUNK - l]
        for part, cols in ((a_r * bb_re - a_i * bb_im, slice(0, n_st)),
                           (a_r * bb_im + a_i * bb_re, slice(n_st, 2 * n_st))):
            hi = part.astype(BF16)
            p_ref[blk, cols] = hi
            plo_ref[blk, cols] = (part - hi.astype(F32)).astype(BF16)
        a_r, a_i = ap_re[l + 1:l + 2], ap_im[l + 1:l + 2]
        qt_ref[blk, :n_st] = (a_r * ct_re - a_i * ct_im).astype(BF16)
        qt_ref[blk, n_st:] = (-(a_r * ct_im + a_i * ct_re)).astype(BF16)

    c_nt = jnp.concatenate([ct_re, -ct_im], axis=-1)
    c_hi = c_nt.astype(BF16)
    c_lo = (c_nt - c_hi.astype(F32)).astype(BF16)
    imp_hi = lax.dot_general(p_ref[...], jnp.concatenate([c_hi, c_lo], axis=0), nt, preferred_element_type=F32)
    imp = (imp_hi[:, :LANES] + imp_hi[:, LANES:]
           + lax.dot_general(plo_ref[...], c_hi, nt, preferred_element_type=F32)).astype(BF16)
    toep_ref[:, LANES:] = imp
    toep_ref[:(SSM_CHUNK - 1) * LANES, :LANES] = imp[LANES:]
    toep_ref[(SSM_CHUNK - 1) * LANES:, :LANES] = jnp.zeros((LANES, LANES), BF16)

    seq_rows = st_ref.shape[1] // bsz
    z = _dot(u, p_ref[...])
    for k in range(n_lt):
        for b in range(bsz):
            st_ref[k, b * seq_rows:b * seq_rows + n_chunk, :] = (
                z[b * n_chunk:(b + 1) * n_chunk, k * LANES:(k + 1) * LANES])
    a_re = [ap_re[SSM_CHUNK:SSM_CHUNK + 1, k * LANES:(k + 1) * LANES] for k in range(half)]
    a_im = [ap_im[SSM_CHUNK:SSM_CHUNK + 1, k * LANES:(k + 1) * LANES] for k in range(half)]
    s_re = [jnp.zeros((bsz, LANES), F32)] * half
    s_im = [jnp.zeros((bsz, LANES), F32)] * half
    for c in range(n_chunk):
        rows = pl.ds(c, bsz, stride=seq_rows)
        for k in range(half):
            z_re = st_ref[k, rows, :]
            z_im = st_ref[half + k, rows, :]
            st_ref[k, rows, :] = s_re[k]
            st_ref[half + k, rows, :] = s_im[k]
            s_re[k], s_im[k] = (a_re[k] * s_re[k] - a_im[k] * s_im[k] + z_re,
                                a_re[k] * s_im[k] + a_im[k] * s_re[k] + z_im)
    st = jnp.concatenate(
        [jnp.concatenate([st_ref[k, b * seq_rows:b * seq_rows + n_chunk, :] for b in range(bsz)], axis=0)
         .astype(BF16) for k in range(n_lt)], axis=-1)

    def finish(cols, y):
        y = y + d_ref[:, cols] * u[:, cols].astype(F32)
        o_ref[:, cols] = jax.nn.gelu(y).astype(BF16)

    n_tile = u.shape[1] // MXU_DIM
    pending = None
    for n in range(n_tile):
        cols = slice(n * MXU_DIM, (n + 1) * MXU_DIM)
        y = _dot(u[:, :(n + 1) * MXU_DIM], toep_ref[(n_tile - 1 - n) * MXU_DIM:, :])
        y = y + lax.dot_general(st, qt_ref[cols, :], nt, preferred_element_type=F32)
        if pending is not None:
            finish(*pending)
        pending = (cols, y)
    finish(*pending)


def _s5_core(u_cm, lam_rows, bt, ct, d_cm, bsz):
    n_slab, rows, kdim = u_cm.shape
    n_st = lam_rows.shape[2]
    slab = lambda shape: pl.BlockSpec((None,) + shape, lambda j: (j,) + (0,) * len(shape))
    return pl.pallas_call(
        functools.partial(_s5_core_kernel, bsz=bsz, n_chunk=rows // bsz),
        grid=(n_slab,),
        in_specs=[slab((rows, kdim)), slab(lam_rows.shape[1:]), slab(bt.shape[1:]), slab(ct.shape[1:]),
                  slab(d_cm.shape[1:])],
        out_specs=slab((rows, kdim)),
        out_shape=jax.ShapeDtypeStruct(u_cm.shape, BF16),
        scratch_shapes=[pltpu.VMEM((2 * n_st // LANES, rows + SUBLANES * bsz, LANES), F32),
                        pltpu.VMEM((kdim, 2 * n_st), BF16),
                        pltpu.VMEM((kdim, 2 * n_st), BF16),
                        pltpu.VMEM((kdim, 2 * n_st), BF16),
                        pltpu.VMEM((kdim, 2 * LANES), BF16)],
        compiler_params=_params("arbitrary"),
    )(u_cm, lam_rows, bt, ct, d_cm)


def _s5_out_kernel(x_ref, mod_ref, gpost_ref, g_ref, w_ref, o_ref, g_nat_ref, *, k):
    n_slab, rows, _ = g_ref.shape
    for l in range(SSM_CHUNK):
        for j in range(n_slab):
            g_nat_ref[j, pl.ds(l, rows, stride=SSM_CHUNK), :] = (
                g_ref[j, :, l * LANES:(l + 1) * LANES].astype(F32))
    g = jnp.concatenate([g_nat_ref[j].astype(BF16) for j in range(n_slab)], axis=-1)
    ab = _dot(g, w_ref[...])
    d = ab.shape[1] // 2
    y = ab[:, :d] * _sigmoid(ab[:, d:])
    o_ref[...] = _post(x_ref[...], y, _mod_rows(mod_ref), gpost_ref[k:k + 1, :], 1.0)


def _s5_out(x, mod, norm_post, sub, g_cm, w_glu):
    bsz, s, d = x.shape
    n_slab = g_cm.shape[0]
    tm = min(ROW_TILE, s)
    rows = tm // SSM_CHUNK
    tiles = s // tm
    return pl.pallas_call(
        functools.partial(_s5_out_kernel, k=sub[1]),
        grid=(bsz, tiles),
        in_specs=[_x_spec(tm, d), _mod_spec(sub, mod), _norm_spec(sub, d),
                  pl.BlockSpec((n_slab, rows, SSM_CHUNK * LANES), lambda b, i: (0, b * tiles + i, 0)),
                  _const_spec(w_glu.shape)],
        out_specs=_x_spec(tm, d),
        out_shape=jax.ShapeDtypeStruct(x.shape, F32),
        scratch_shapes=[pltpu.VMEM((n_slab, tm, LANES), F32)],
        compiler_params=_params("arbitrary", "arbitrary"),
    )(x, mod, norm_post, g_cm, w_glu)


def _s5_slab_params(lam_re, lam_im, b_re, b_im, c_re, c_im, d_skip, log_dt):
    n_group, n_state = lam_re.shape
    n_ch = b_re.shape[2]
    n_slab = n_group // SLAB_GROUPS
    n_st = SLAB_GROUPS * n_state
    log_dt_st = jnp.broadcast_to(log_dt[:, None], (n_group, n_state))
    lam_rows = jnp.stack([v.reshape(n_slab, n_st) for v in (lam_re, lam_im, log_dt_st)], axis=1)
    same_group = jnp.eye(SLAB_GROUPS, dtype=bool)[None, :, None, :, None]

    def block_diag(w_re, w_im):
        parts = []
        for w in (w_re, w_im):
            w = w.reshape(n_slab, SLAB_GROUPS, n_ch, 1, n_state)
            parts.append(jnp.where(same_group, w, 0.0).reshape(n_slab, LANES, n_st))
        return jnp.stack(parts, axis=1)

    bt = block_diag(b_re.transpose(0, 2, 1), b_im.transpose(0, 2, 1))
    ct = block_diag(c_re, c_im)
    d_cm = jnp.tile(d_skip.reshape(n_slab, 1, LANES), (1, 1, SSM_CHUNK))
    return lam_rows, bt, ct, d_cm


def kernel(x, c, ada_w, ada_b, norm_pre, norm_post, ffn_w_in, ffn_w_out, ab_w_in, pool_w, pool_scale, sgu_ln_g, sgu_ln_b, sgu_w, sgu_b, ab_w_out, ssm_w_in, ssm_lam_re, ssm_lam_im, ssm_b_re, ssm_b_im, ssm_c_re, ssm_c_im, ssm_d, ssm_log_dt, ssm_w_glu):
    bsz, s, d = x.shape
    depth = ada_w.shape[0]
    assert s % FFN_ROW_TILE == 0 or s < ROW_TILE
    mod = _ada_mod(c, ada_w, ada_b)

    def ffn_casts(l, half):
        return [(ffn_w_in, (l, half)), (ffn_w_out, (l, half))]

    def mixer_casts(l):
        i = l // 2
        return [(ab_w_in, (i,)), (ab_w_out, (i,))] if l % 2 == 0 else [(ssm_w_in, (i,)), (ssm_w_glu, (i,))]

    w_ffn = [ffn_w_in[0, 0].astype(BF16), ffn_w_out[0, 0].astype(BF16)]
    for l in range(depth):
        i = l // 2
        x, cast = _ffn(x, mod, norm_pre, norm_post, (l, 0), *w_ffn, casts=mixer_casts(l) + ffn_casts(l, 1))
        w_mix, w_ffn = cast[:2], cast[2:]
        if l % 2 == 0:
            x = _mix0(x, mod, norm_pre, norm_post, (l, 1), w_mix[0], pool_w[i], pool_scale[i],
                      sgu_ln_g[i], sgu_ln_b[i], sgu_w[i], sgu_b[i], w_mix[1])
        else:
            slab_params = _s5_slab_params(ssm_lam_re[i], ssm_lam_im[i], ssm_b_re[i], ssm_b_im[i],
                                          ssm_c_re[i], ssm_c_im[i], ssm_d[i], ssm_log_dt[i])
            u_cm = _s5_in(x, mod, norm_pre, (l, 1), w_mix[0])
            g_cm = _s5_core(u_cm, *slab_params, bsz)
            x = _s5_out(x, mod, norm_post, (l, 1), g_cm, w_mix[1])
        x, w_ffn = _ffn(x, mod, norm_pre, norm_post, (l, 2), *w_ffn,
                        casts=ffn_casts(l + 1, 0) if l + 1 < depth else [])
    return x
```

```python
import functools

import jax
import jax.numpy as jnp
from jax import lax
from jax.experimental import pallas as pl
from jax.experimental.pallas import tpu as pltpu

F32 = jnp.float32
BF16 = jnp.bfloat16
EPS = 1e-6
N_SUB = 3
FFN_RES_WEIGHT = 0.5
POOL_WINDOWS = (2, 4, 8, 16)
SGU_HEADS = 4
SGU_CHUNK = 128
SSM_GROUP = 16
SSM_CHUNK = 8
LANES = 128
SUBLANES = 8
SLAB_GROUPS = LANES // SSM_GROUP
MXU_DIM = 256
VMEM_LIMIT_BYTES = 56 * 1024 * 1024
ROW_TILE = 512
FFN_ROW_TILE = 1024
FFN_SPLIT = 2


def _const_spec(shape, lead=()):
    nd = len(shape)
    return pl.BlockSpec((None,) * len(lead) + tuple(shape), lambda *_: tuple(lead) + (0,) * nd,
                        pipeline_mode=pl.Buffered(1))


def _params(*sem):
    return pltpu.CompilerParams(dimension_semantics=sem, vmem_limit_bytes=VMEM_LIMIT_BYTES)


def _rms(x, g):
    return x * lax.rsqrt(jnp.mean(x * x, axis=-1, keepdims=True) + EPS) * g


def _mod_rows(mod_ref):
    b = pl.program_id(0)
    return [mod_ref[r, pl.ds(b, 1), :] for r in range(3)]


def _pre(x, mod, g_pre):
    return _rms(x, g_pre * (1.0 + mod[1])) + mod[0]


def _post(x, y, mod, g_post, res_weight):
    return x + _rms(y, res_weight * mod[2] * g_post)


def _sigmoid(v):
    return 1.0 / (1.0 + jnp.exp(-v))


def _dot(a, b):
    return jnp.dot(a, b, preferred_element_type=F32)


def _ada_kernel(c_ref, w_ref, b_ref, o_ref):
    c = c_ref[...]
    cond = (c * _sigmoid(c)).astype(BF16)
    o_ref[...] = _dot(cond, w_ref[...].astype(BF16)) + b_ref[...]


def _ada_mod(c, ada_w, ada_b):
    depth, d, n = ada_w.shape
    bsz = c.shape[0]
    out = pl.pallas_call(
        _ada_kernel,
        grid=(depth, n // d),
        in_specs=[
            pl.BlockSpec((bsz, d), lambda l, j: (0, 0)),
            pl.BlockSpec((None, d, d), lambda l, j: (l, 0, j)),
            pl.BlockSpec((None, 1, d), lambda l, j: (l, 0, j)),
        ],
        out_specs=pl.BlockSpec((None, None, bsz, d), lambda l, j: (l, j, 0, 0)),
        out_shape=jax.ShapeDtypeStruct((depth, n // d, bsz, d), F32),
        compiler_params=_params("arbitrary", "arbitrary"),
    )(c, ada_w, ada_b.reshape(depth, 1, n))
    return out.reshape(depth, N_SUB, 3, bsz, d)


def _x_spec(tm, d):
    return pl.BlockSpec((None, tm, d), lambda b, i: (b, i, 0))


def _mod_spec(sub, mod):
    return _const_spec(mod.shape[2:], lead=sub)


def _norm_spec(sub, d):
    return _const_spec((N_SUB, d), lead=sub[:1])


def _ffn_kernel(x_ref, mod_ref, gpre_ref, gpost_ref, win_ref, wout_ref, *rest, k, d_ff, fc, n_split):
    n_cast = len(rest) // 2
    o_ref = rest[n_cast]
    for src_ref, dst_ref in zip(rest[:n_cast], rest[n_cast + 1:]):
        dst_ref[...] = src_ref[...].astype(BF16)

    mod = _mod_rows(mod_ref)
    sub = x_ref.shape[0] // n_split
    rows = [slice(p * sub, (p + 1) * sub) for p in range(n_split)]
    h = [_pre(x_ref[r, :], mod, gpre_ref[k:k + 1, :]).astype(BF16) for r in rows]
    acc = [jnp.zeros((sub, x_ref.shape[1]), F32)] * n_split
    for j in range(d_ff // fc):
        for p in range(n_split):
            a = _dot(h[p], win_ref[:, j * fc:(j + 1) * fc])
            b = _dot(h[p], win_ref[:, d_ff + j * fc:d_ff + (j + 1) * fc])
            act = (a * _sigmoid(a) * b).astype(BF16)
            acc[p] = acc[p] + _dot(act, wout_ref[j * fc:(j + 1) * fc, :])
    for p, r in enumerate(rows):
        o_ref[r, :] = _post(x_ref[r, :], acc[p], mod, gpost_ref[k:k + 1, :], FFN_RES_WEIGHT)


def _ffn(x, mod, norm_pre, norm_post, sub, w_in, w_out, casts=()):
    bsz, s, d = x.shape
    d_ff = w_out.shape[0]
    tm = min(FFN_ROW_TILE, s)
    tiles = s // tm
    steps = bsz * tiles
    cast_in, cast_out, cast_shapes = [], [], []
    for w, lead in casts:
        r, c = w.shape[len(lead):]
        cast_in.append(pl.BlockSpec((None,) * len(lead) + (r // steps, c),
                                    lambda b, i, lead=lead: lead + (b * tiles + i, 0)))
        cast_out.append(pl.BlockSpec((r // steps, c), lambda b, i: (b * tiles + i, 0)))
        cast_shapes.append(jax.ShapeDtypeStruct((r, c), BF16))
    out = pl.pallas_call(
        functools.partial(_ffn_kernel, k=sub[1], d_ff=d_ff, fc=MXU_DIM, n_split=FFN_SPLIT),
        grid=(bsz, tiles),
        in_specs=[_x_spec(tm, d), _mod_spec(sub, mod), _norm_spec(sub, d), _norm_spec(sub, d),
                  _const_spec(w_in.shape), _const_spec(w_out.shape)] + cast_in,
        out_specs=[_x_spec(tm, d)] + cast_out,
        out_shape=[jax.ShapeDtypeStruct(x.shape, F32)] + cast_shapes,
        compiler_params=_params("arbitrary", "arbitrary"),
    )(x, mod, norm_pre, norm_post, w_in, w_out, *[w for w, _ in casts])
    return out[0], list(out[1:])


def _mix0_kernel(x_ref, mod_ref, gpre_ref, gpost_ref, win_ref, poolw_ref, pscale_ref, lng_ref, lnb_ref,
                 sguw_ref, sgub_ref, wout_ref, o_ref, carry_ref, *, k, pool_width):
    i = pl.program_id(1)
    tm = x_ref.shape[0]
    n_pool = len(POOL_WINDOWS)
    gd = pool_width // n_pool

    @pl.when(i == 0)
    def _():
        carry_ref[...] = jnp.zeros(carry_ref.shape, F32)

    x = x_ref[...]
    mod = _mod_rows(mod_ref)
    h = _pre(x, mod, gpre_ref[k:k + 1, :]).astype(BF16)
    z = _dot(h, win_ref[...])

    a = z[:, :pool_width]
    row8 = lax.broadcasted_iota(jnp.int32, (8, 1), 0)

    def shifted(cur, level, k):
        lanes = cur.shape[1]
        prev8 = carry_ref[level, :, pool_width - lanes:]
        carry_ref[level, :, pool_width - lanes:] = cur[tm - 8:, :]
        rolled = pltpu.roll(cur, k, axis=0)
        if k == 8:
            return jnp.concatenate([prev8, rolled[8:]], axis=0)
        head = jnp.where(row8 < k, pltpu.roll(prev8, k, axis=0), rolled[:8])
        return jnp.concatenate([head, rolled[8:]], axis=0)

    pos = (i * tm + 1 + lax.broadcasted_iota(jnp.int32, (tm, 1), 0)).astype(F32)
    level_sum = a
    y_a = []
    for g, w in enumerate(POOL_WINDOWS):
        level_sum = level_sum + shifted(level_sum, g, w // 2)
        mean = level_sum[:, :gd] / jnp.minimum(pos, float(w))
        dg = (mean - a[:, g * gd:(g + 1) * gd]).astype(BF16)
        y_a.append(_dot(dg, poolw_ref[g]) * pscale_ref[:, g * gd:(g + 1) * gd])
        level_sum = level_sum[:, gd:]
    y_a = jnp.concatenate(y_a, axis=-1).astype(BF16)

    zb = jax.nn.gelu(z[:, pool_width:])
    sgu_width = zb.shape[1] // 2
    hd = sgu_width // SGU_HEADS
    u = zb[:, :sgu_width]
    v = zb[:, sgu_width:]
    y_b = []
    for hh in range(SGU_HEADS):
        vh = v[:, hh * hd:(hh + 1) * hd]
        mu = jnp.mean(vh, axis=-1, keepdims=True)
        var = jnp.mean(jnp.square(vh - mu), axis=-1, keepdims=True)
        vn = (vh - mu) * lax.rsqrt(var + EPS) * lng_ref[:, hh * hd:(hh + 1) * hd] + lnb_ref[:, hh * hd:(hh + 1) * hd]
        vn = vn.astype(BF16)
        s_h = [_dot(sguw_ref[hh], vn[ck * SGU_CHUNK:(ck + 1) * SGU_CHUNK]) + sgub_ref[hh]
               for ck in range(tm // SGU_CHUNK)]
        y_b.append(u[:, hh * hd:(hh + 1) * hd] * jnp.concatenate(s_h, axis=0))
    y_b = jnp.concatenate(y_b, axis=-1).astype(BF16)

    y = _dot(y_a, wout_ref[:pool_width, :]) + _dot(y_b, wout_ref[pool_width:, :])
    o_ref[...] = _post(x, y, mod, gpost_ref[k:k + 1, :], 1.0)


def _mix0(x, mod, norm_pre, norm_post, sub, w_in, pool_w, pool_scale, ln_g, ln_b, sgu_w, sgu_b, w_out):
    bsz, s, d = x.shape
    pool_width = pool_scale.shape[0]
    sgu_width = ln_g.shape[0]
    hd = sgu_width // SGU_HEADS
    tm = min(ROW_TILE, s)
    causal = jnp.tril(jnp.ones((SGU_CHUNK, SGU_CHUNK), dtype=bool))
    sgu_w_causal = jnp.where(causal[None], sgu_w, 0.0).astype(BF16)
    sgu_b_rows = jnp.broadcast_to(sgu_b[:, :, None], (SGU_HEADS, SGU_CHUNK, hd))
    return pl.pallas_call(
        functools.partial(_mix0_kernel, k=sub[1], pool_width=pool_width),
        grid=(bsz, s // tm),
        in_specs=[_x_spec(tm, d), _mod_spec(sub, mod), _norm_spec(sub, d), _norm_spec(sub, d),
                  _const_spec(w_in.shape), _const_spec(pool_w.shape), _const_spec((1, pool_width)),
                  _const_spec((1, sgu_width)), _const_spec((1, sgu_width)),
                  _const_spec(sgu_w.shape), _const_spec(sgu_b_rows.shape), _const_spec(w_out.shape)],
        out_specs=_x_spec(tm, d),
        out_shape=jax.ShapeDtypeStruct(x.shape, F32),
        scratch_shapes=[pltpu.VMEM((len(POOL_WINDOWS), 8, pool_width), F32)],
        compiler_params=_params("arbitrary", "arbitrary"),
    )(x, mod, norm_pre, norm_post, w_in, pool_w.astype(BF16),
      pool_scale.reshape(1, pool_width), ln_g.reshape(1, sgu_width), ln_b.reshape(1, sgu_width),
      sgu_w_causal, sgu_b_rows, w_out)


def _s5_in_kernel(x_ref, mod_ref, gpre_ref, w_ref, o_ref, u_ref, *, k):
    n_slab, rows, _ = o_ref.shape
    h = _pre(x_ref[...], _mod_rows(mod_ref), gpre_ref[k:k + 1, :]).astype(BF16)
    u = _dot(h, w_ref[...])
    for j in range(n_slab):
        u_ref[j] = u[:, j * LANES:(j + 1) * LANES]
    for l in range(SSM_CHUNK):
        for j in range(n_slab):
            o_ref[j, :, l * LANES:(l + 1) * LANES] = (
                u_ref[j, pl.ds(l, rows, stride=SSM_CHUNK), :].astype(BF16))


def _s5_in(x, mod, norm_pre, sub, w_in):
    bsz, s, d = x.shape
    width = w_in.shape[1]
    n_slab = width // LANES
    tm = min(ROW_TILE, s)
    rows = tm // SSM_CHUNK
    tiles = s // tm
    return pl.pallas_call(
        functools.partial(_s5_in_kernel, k=sub[1]),
        grid=(bsz, tiles),
        in_specs=[_x_spec(tm, d), _mod_spec(sub, mod), _norm_spec(sub, d), _const_spec(w_in.shape)],
        out_specs=pl.BlockSpec((n_slab, rows, SSM_CHUNK * LANES), lambda b, i: (0, b * tiles + i, 0)),
        out_shape=jax.ShapeDtypeStruct((n_slab, bsz * s // SSM_CHUNK, SSM_CHUNK * LANES), BF16),
        scratch_shapes=[pltpu.VMEM((n_slab, tm, LANES), F32)],
        compiler_params=_params("arbitrary", "arbitrary"),
    )(x, mod, norm_pre, w_in)


def _s5_core_kernel(u_ref, lam_ref, bt_ref, ct_ref, d_ref, o_ref, st_ref, p_ref, plo_ref, qt_ref, toep_ref,
                    *, bsz, n_chunk):
    n_lt = st_ref.shape[0]
    half = n_lt // 2
    n_st = half * LANES
    u = u_ref[...]

    lam_re, lam_im = lam_ref[0:1], lam_ref[1:2]
    dt = jnp.exp(lam_ref[2:3])
    tau = lax.broadcasted_iota(jnp.int32, (SSM_CHUNK + 8, 1), 0).astype(F32)
    mag = jnp.exp(lam_re * dt * tau)
    ang = lam_im * dt * tau
    ap_re, ap_im = mag * jnp.cos(ang), mag * jnp.sin(ang)
    num_re, num_im = ap_re[1:2] - 1.0, ap_im[1:2]
    den = lam_re * lam_re + lam_im * lam_im
    cf_re = (num_re * lam_re + num_im * lam_im) / den
    cf_im = (num_im * lam_re - num_re * lam_im) / den
    bt_re, bt_im = bt_ref[0], bt_ref[1]
    bb_re, bb_im = cf_re * bt_re - cf_im * bt_im, cf_re * bt_im + cf_im * bt_re
    ct_re, ct_im = ct_ref[0], ct_ref[1]

    nt = (((1,), (1,)), ((), ()))
    for l in range(SSM_CHUNK):
        blk = slice(l * LANES, (l + 1) * LANES)
        a_r, a_i = ap_re[SSM_CHUNK - 1 - l:SSM_CHUNK - l], ap_im[SSM_CHUNK - 1 - l:SSM_CHUNK - l]
        for part, cols in ((a_r * bb_re - a_i * bb_im, slice(0, n_st)),
                           (a_r * bb_im + a_i * bb_re, slice(n_st, 2 * n_st))):
            hi = part.astype(BF16)
            p_ref[blk, cols] = hi
            plo_ref[blk, cols] = (part - hi.astype(F32)).astype(BF16)

    c_nt = jnp.concatenate([ct_re, -ct_im], axis=-1)
    c_hi = c_nt.astype(BF16)
    c_lo = (c_nt - c_hi.astype(F32)).astype(BF16)
    imp_hi = lax.dot_general(p_ref[...], jnp.concatenate([c_hi, c_lo], axis=0), nt, preferred_element_type=F32)
    imp = (imp_hi[:, :LANES] + imp_hi[:, LANES:]
           + lax.dot_general(plo_ref[...], c_hi, nt, preferred_element_type=F32))
    lag0 = (SSM_CHUNK - 1) * LANES
    toep_ref[:, LANES:] = imp.astype(BF16)
    toep_ref[:lag0, :LANES] = imp[LANES:].astype(BF16)
    toep_ref[lag0:, :LANES] = jnp.zeros((LANES, LANES), BF16)

    seq_rows = st_ref.shape[1] // bsz
    z = _dot(u, p_ref[...])
    for k in range(n_lt):
        for b in range(bsz):
            st_ref[k, b * seq_rows:b * seq_rows + n_chunk, :] = (
                z[b * n_chunk:(b + 1) * n_chunk, k * LANES:(k + 1) * LANES])

    n_tile = u.shape[1] // MXU_DIM
    col_tiles = [slice(n * MXU_DIM, (n + 1) * MXU_DIM) for n in range(n_tile)]
    d_row = jnp.concatenate([d_ref[...]] * (MXU_DIM // LANES), axis=-1)
    y_local = [_dot(u[:, :(n + 1) * MXU_DIM], toep_ref[(n_tile - 1 - n) * MXU_DIM:, :])
               + d_row * u[:, cols].astype(F32) for n, cols in enumerate(col_tiles)]

    for l in range(SSM_CHUNK):
        blk = slice(l * LANES, (l + 1) * LANES)
        a_r, a_i = ap_re[l + 1:l + 2], ap_im[l + 1:l + 2]
        qt_ref[blk, :n_st] = (a_r * ct_re - a_i * ct_im).astype(BF16)
        qt_ref[blk, n_st:] = (-(a_r * ct_im + a_i * ct_re)).astype(BF16)

    a_re = [ap_re[SSM_CHUNK:SSM_CHUNK + 1, k * LANES:(k + 1) * LANES] for k in range(half)]
    a_im = [ap_im[SSM_CHUNK:SSM_CHUNK + 1, k * LANES:(k + 1) * LANES] for k in range(half)]
    s_re = [jnp.zeros((bsz, LANES), F32)] * half
    s_im = [jnp.zeros((bsz, LANES), F32)] * half
    for c in range(n_chunk):
        rows = pl.ds(c, bsz, stride=seq_rows)
        for k in range(half):
            z_re = st_ref[k, rows, :]
            z_im = st_ref[half + k, rows, :]
            st_ref[k, rows, :] = s_re[k]
            st_ref[half + k, rows, :] = s_im[k]
            s_re[k], s_im[k] = (a_re[k] * s_re[k] - a_im[k] * s_im[k] + z_re,
                                a_re[k] * s_im[k] + a_im[k] * s_re[k] + z_im)
    st = jnp.concatenate(
        [jnp.concatenate([st_ref[k, b * seq_rows:b * seq_rows + n_chunk, :] for b in range(bsz)], axis=0)
         .astype(BF16) for k in range(n_lt)], axis=-1)

    for n, cols in enumerate(col_tiles):
        y = y_local[n] + lax.dot_general(st, qt_ref[cols, :], nt, preferred_element_type=F32)
        o_ref[:, cols] = jax.nn.gelu(y).astype(BF16)


def _s5_core(u_cm, lam_rows, bt, ct, d_cm, bsz):
    n_slab, rows, kdim = u_cm.shape
    n_st = lam_rows.shape[2]
    slab = lambda shape: pl.BlockSpec((None,) + shape, lambda j: (j,) + (0,) * len(shape))
    return pl.pallas_call(
        functools.partial(_s5_core_kernel, bsz=bsz, n_chunk=rows // bsz),
        grid=(n_slab,),
        in_specs=[slab((rows, kdim)), slab(lam_rows.shape[1:]), slab(bt.shape[1:]), slab(ct.shape[1:]),
                  slab(d_cm.shape[1:])],
        out_specs=slab((rows, kdim)),
        out_shape=jax.ShapeDtypeStruct(u_cm.shape, BF16),
        scratch_shapes=[pltpu.VMEM((2 * n_st // LANES, rows + SUBLANES * bsz, LANES), F32),
                        pltpu.VMEM((kdim, 2 * n_st), BF16),
                        pltpu.VMEM((kdim, 2 * n_st), BF16),
                        pltpu.VMEM((kdim, 2 * n_st), BF16),
                        pltpu.VMEM((kdim, 2 * LANES), BF16)],
        compiler_params=_params("arbitrary"),
    )(u_cm, lam_rows, bt, ct, d_cm)


def _s5_out_kernel(x_ref, mod_ref, gpost_ref, g_ref, w_ref, o_ref, g_nat_ref, *, k):
    n_slab, rows, _ = g_ref.shape
    for l in range(SSM_CHUNK):
        for j in range(n_slab):
            g_nat_ref[j, pl.ds(l, rows, stride=SSM_CHUNK), :] = (
                g_ref[j, :, l * LANES:(l + 1) * LANES].astype(F32))
    g = jnp.concatenate([g_nat_ref[j].astype(BF16) for j in range(n_slab)], axis=-1)
    ab = _dot(g, w_ref[...])
    d = ab.shape[1] // 2
    y = ab[:, :d] * _sigmoid(ab[:, d:])
    o_ref[...] = _post(x_ref[...], y, _mod_rows(mod_ref), gpost_ref[k:k + 1, :], 1.0)


def _s5_out(x, mod, norm_post, sub, g_cm, w_glu):
    bsz, s, d = x.shape
    n_slab = g_cm.shape[0]
    tm = min(ROW_TILE, s)
    rows = tm // SSM_CHUNK
    tiles = s // tm
    return pl.pallas_call(
        functools.partial(_s5_out_kernel, k=sub[1]),
        grid=(bsz, tiles),
        in_specs=[_x_spec(tm, d), _mod_spec(sub, mod), _norm_spec(sub, d),
                  pl.BlockSpec((n_slab, rows, SSM_CHUNK * LANES), lambda b, i: (0, b * tiles + i, 0)),
                  _const_spec(w_glu.shape)],
        out_specs=_x_spec(tm, d),
        out_shape=jax.ShapeDtypeStruct(x.shape, F32),
        scratch_shapes=[pltpu.VMEM((n_slab, tm, LANES), F32)],
        compiler_params=_params("arbitrary", "arbitrary"),
    )(x, mod, norm_post, g_cm, w_glu)


def _s5_slab_params(lam_re, lam_im, b_re, b_im, c_re, c_im, d_skip, log_dt):
    n_group, n_state = lam_re.shape
    n_ch = b_re.shape[2]
    n_slab = n_group // SLAB_GROUPS
    n_st = SLAB_GROUPS * n_state
    log_dt_st = jnp.broadcast_to(log_dt[:, None], (n_group, n_state))
    lam_rows = jnp.stack([v.reshape(n_slab, n_st) for v in (lam_re, lam_im, log_dt_st)], axis=1)
    same_group = jnp.eye(SLAB_GROUPS, dtype=bool)[None, :, None, :, None]

    def block_diag(w_re, w_im):
        parts = []
        for w in (w_re, w_im):
            w = w.reshape(n_slab, SLAB_GROUPS, n_ch, 1, n_state)
            parts.append(jnp.where(same_group, w, 0.0).reshape(n_slab, LANES, n_st))
        return jnp.stack(parts, axis=1)

    bt = block_diag(b_re.transpose(0, 2, 1), b_im.transpose(0, 2, 1))
    ct = block_diag(c_re, c_im)
    return lam_rows, bt, ct, d_skip.reshape(n_slab, 1, LANES)


def kernel(x, c, ada_w, ada_b, norm_pre, norm_post, ffn_w_in, ffn_w_out, ab_w_in, pool_w, pool_scale, sgu_ln_g, sgu_ln_b, sgu_w, sgu_b, ab_w_out, ssm_w_in, ssm_lam_re, ssm_lam_im, ssm_b_re, ssm_b_im, ssm_c_re, ssm_c_im, ssm_d, ssm_log_dt, ssm_w_glu):
    bsz, s, d = x.shape
    depth = ada_w.shape[0]
    assert s % FFN_ROW_TILE == 0 or s < ROW_TILE
    mod = _ada_mod(c, ada_w, ada_b)

    def ffn_casts(l, half):
        return [(ffn_w_in, (l, half)), (ffn_w_out, (l, half))]

    def mixer_casts(l):
        i = l // 2
        return [(ab_w_in, (i,)), (ab_w_out, (i,))] if l % 2 == 0 else [(ssm_w_in, (i,)), (ssm_w_glu, (i,))]

    w_ffn = [ffn_w_in[0, 0].astype(BF16), ffn_w_out[0, 0].astype(BF16)]
    for l in range(depth):
        i = l // 2
        x, cast = _ffn(x, mod, norm_pre, norm_post, (l, 0), *w_ffn, casts=mixer_casts(l) + ffn_casts(l, 1))
        w_mix, w_ffn = cast[:2], cast[2:]
        if l % 2 == 0:
            x = _mix0(x, mod, norm_pre, norm_post, (l, 1), w_mix[0], pool_w[i], pool_scale[i],
                      sgu_ln_g[i], sgu_ln_b[i], sgu_w[i], sgu_b[i], w_mix[1])
        else:
            slab_params = _s5_slab_params(ssm_lam_re[i], ssm_lam_im[i], ssm_b_re[i], ssm_b_im[i],
                                          ssm_c_re[i], ssm_c_im[i], ssm_d[i], ssm_log_dt[i])
            u_cm = _s5_in(x, mod, norm_pre, (l, 1), w_mix[0])
            g_cm = _s5_core(u_cm, *slab_params, bsz)
            x = _s5_out(x, mod, norm_post, (l, 1), g_cm, w_mix[1])
        x, w_ffn = _ffn(x, mod, norm_pre, norm_post, (l, 2), *w_ffn,
                        casts=ffn_casts(l + 1, 0) if l + 1 < depth else [])
    return x
```

```python
import functools

import jax
import jax.numpy as jnp
from jax import lax
from jax.experimental import pallas as pl
from jax.experimental.pallas import tpu as pltpu

F32 = jnp.float32
BF16 = jnp.bfloat16
EPS = 1e-6
N_SUB = 3
FFN_RES_WEIGHT = 0.5
POOL_WINDOWS = (2, 4, 8, 16)
SGU_HEADS = 4
SGU_CHUNK = 128
SSM_GROUP = 16
SSM_CHUNK = 8
LANES = 128
SUBLANES = 8
SLAB_GROUPS = LANES // SSM_GROUP
MXU_DIM = 256
VMEM_LIMIT_BYTES = 56 * 1024 * 1024
ROW_TILE = 512
S5_ROW_TILE = 1024
FFN_ROW_TILE = 1024
FFN_SPLIT = 2


def _const_spec(shape, lead=()):
    nd = len(shape)
    return pl.BlockSpec((None,) * len(lead) + tuple(shape), lambda *_: tuple(lead) + (0,) * nd,
                        pipeline_mode=pl.Buffered(1))


def _params(*sem):
    return pltpu.CompilerParams(dimension_semantics=sem, vmem_limit_bytes=VMEM_LIMIT_BYTES)


def _rms(x, g):
    return x * lax.rsqrt(jnp.mean(x * x, axis=-1, keepdims=True) + EPS) * g


def _mod_rows(mod_ref):
    b = pl.program_id(0)
    return [mod_ref[r, pl.ds(b, 1), :] for r in range(3)]


def _pre(x, mod, g_pre):
    return _rms(x, g_pre * (1.0 + mod[1])) + mod[0]


def _post(x, y, mod, g_post, res_weight):
    return x + _rms(y, res_weight * mod[2] * g_post)


def _sigmoid(v):
    return 1.0 / (1.0 + jnp.exp(-v))


def _dot(a, b):
    return jnp.dot(a, b, preferred_element_type=F32)


def _ada_kernel(c_ref, w_ref, b_ref, o_ref):
    c = c_ref[...]
    cond = (c * _sigmoid(c)).astype(BF16)
    o_ref[...] = _dot(cond, w_ref[...].astype(BF16)) + b_ref[...]


def _ada_mod(c, ada_w, ada_b):
    depth, d, n = ada_w.shape
    bsz = c.shape[0]
    out = pl.pallas_call(
        _ada_kernel,
        grid=(depth, n // d),
        in_specs=[
            pl.BlockSpec((bsz, d), lambda l, j: (0, 0)),
            pl.BlockSpec((None, d, d), lambda l, j: (l, 0, j)),
            pl.BlockSpec((None, 1, d), lambda l, j: (l, 0, j)),
        ],
        out_specs=pl.BlockSpec((None, None, bsz, d), lambda l, j: (l, j, 0, 0)),
        out_shape=jax.ShapeDtypeStruct((depth, n // d, bsz, d), F32),
        compiler_params=_params("arbitrary", "arbitrary"),
    )(c, ada_w, ada_b.reshape(depth, 1, n))
    return out.reshape(depth, N_SUB, 3, bsz, d)


def _x_spec(tm, d):
    return pl.BlockSpec((None, tm, d), lambda b, i: (b, i, 0))


def _mod_spec(sub, mod):
    return _const_spec(mod.shape[2:], lead=sub)


def _norm_spec(sub, d):
    return _const_spec((N_SUB, d), lead=sub[:1])


def _ffn_kernel(x_ref, mod_ref, gpre_ref, gpost_ref, win_ref, wout_ref, *rest, k, d_ff, fc, n_split):
    n_cast = len(rest) // 2
    o_ref = rest[n_cast]
    for src_ref, dst_ref in zip(rest[:n_cast], rest[n_cast + 1:]):
        dst_ref[...] = src_ref[...].astype(BF16)

    mod = _mod_rows(mod_ref)
    sub = x_ref.shape[0] // n_split
    rows = [slice(p * sub, (p + 1) * sub) for p in range(n_split)]
    h = [_pre(x_ref[r, :], mod, gpre_ref[k:k + 1, :]).astype(BF16) for r in rows]
    acc = [jnp.zeros((sub, x_ref.shape[1]), F32)] * n_split
    for j in range(d_ff // fc):
        for p in range(n_split):
            a = _dot(h[p], win_ref[:, j * fc:(j + 1) * fc])
            b = _dot(h[p], win_ref[:, d_ff + j * fc:d_ff + (j + 1) * fc])
            act = (a * _sigmoid(a) * b).astype(BF16)
            acc[p] = acc[p] + _dot(act, wout_ref[j * fc:(j + 1) * fc, :])
    for p, r in enumerate(rows):
        o_ref[r, :] = _post(x_ref[r, :], acc[p], mod, gpost_ref[k:k + 1, :], FFN_RES_WEIGHT)


def _ffn(x, mod, norm_pre, norm_post, sub, w_in, w_out, casts=()):
    bsz, s, d = x.shape
    d_ff = w_out.shape[0]
    tm = min(FFN_ROW_TILE, s)
    tiles = s // tm
    steps = bsz * tiles
    cast_in, cast_out, cast_shapes = [], [], []
    for w, lead in casts:
        r, c = w.shape[len(lead):]
        cast_in.append(pl.BlockSpec((None,) * len(lead) + (r // steps, c),
                                    lambda b, i, lead=lead: lead + (b * tiles + i, 0)))
        cast_out.append(pl.BlockSpec((r // steps, c), lambda b, i: (b * tiles + i, 0)))
        cast_shapes.append(jax.ShapeDtypeStruct((r, c), BF16))
    out = pl.pallas_call(
        functools.partial(_ffn_kernel, k=sub[1], d_ff=d_ff, fc=MXU_DIM, n_split=FFN_SPLIT),
        grid=(bsz, tiles),
        in_specs=[_x_spec(tm, d), _mod_spec(sub, mod), _norm_spec(sub, d), _norm_spec(sub, d),
                  _const_spec(w_in.shape), _const_spec(w_out.shape)] + cast_in,
        out_specs=[_x_spec(tm, d)] + cast_out,
        out_shape=[jax.ShapeDtypeStruct(x.shape, F32)] + cast_shapes,
        compiler_params=_params("arbitrary", "arbitrary"),
    )(x, mod, norm_pre, norm_post, w_in, w_out, *[w for w, _ in casts])
    return out[0], list(out[1:])


def _mix0_kernel(x_ref, mod_ref, gpre_ref, gpost_ref, win_ref, poolw_ref, pscale_ref, lng_ref, lnb_ref,
                 sguw_ref, sgub_ref, wout_ref, o_ref, carry_ref, *, k, pool_width):
    i = pl.program_id(1)
    tm = x_ref.shape[0]
    n_pool = len(POOL_WINDOWS)
    gd = pool_width // n_pool

    @pl.when(i == 0)
    def _():
        carry_ref[...] = jnp.zeros(carry_ref.shape, F32)

    x = x_ref[...]
    mod = _mod_rows(mod_ref)
    h = _pre(x, mod, gpre_ref[k:k + 1, :]).astype(BF16)
    z = _dot(h, win_ref[...])

    a = z[:, :pool_width]
    row8 = lax.broadcasted_iota(jnp.int32, (8, 1), 0)

    def shifted(cur, level, k):
        lanes = cur.shape[1]
        prev8 = carry_ref[level, :, pool_width - lanes:]
        carry_ref[level, :, pool_width - lanes:] = cur[tm - 8:, :]
        rolled = pltpu.roll(cur, k, axis=0)
        if k == 8:
            return jnp.concatenate([prev8, rolled[8:]], axis=0)
        head = jnp.where(row8 < k, pltpu.roll(prev8, k, axis=0), rolled[:8])
        return jnp.concatenate([head, rolled[8:]], axis=0)

    pos = (i * tm + 1 + lax.broadcasted_iota(jnp.int32, (tm, 1), 0)).astype(F32)
    level_sum = a
    y_a = []
    for g, w in enumerate(POOL_WINDOWS):
        level_sum = level_sum + shifted(level_sum, g, w // 2)
        mean = level_sum[:, :gd] / jnp.minimum(pos, float(w))
        dg = (mean - a[:, g * gd:(g + 1) * gd]).astype(BF16)
        y_a.append(_dot(dg, poolw_ref[g]) * pscale_ref[:, g * gd:(g + 1) * gd])
        level_sum = level_sum[:, gd:]
    y_a = jnp.concatenate(y_a, axis=-1).astype(BF16)

    zb = jax.nn.gelu(z[:, pool_width:])
    sgu_width = zb.shape[1] // 2
    hd = sgu_width // SGU_HEADS
    u = zb[:, :sgu_width]
    v = zb[:, sgu_width:]
    y_b = []
    for hh in range(SGU_HEADS):
        vh = v[:, hh * hd:(hh + 1) * hd]
        mu = jnp.mean(vh, axis=-1, keepdims=True)
        var = jnp.mean(jnp.square(vh - mu), axis=-1, keepdims=True)
        vn = (vh - mu) * lax.rsqrt(var + EPS) * lng_ref[:, hh * hd:(hh + 1) * hd] + lnb_ref[:, hh * hd:(hh + 1) * hd]
        vn = vn.astype(BF16)
        s_h = [_dot(sguw_ref[hh], vn[ck * SGU_CHUNK:(ck + 1) * SGU_CHUNK]) + sgub_ref[hh]
               for ck in range(tm // SGU_CHUNK)]
        y_b.append(u[:, hh * hd:(hh + 1) * hd] * jnp.concatenate(s_h, axis=0))
    y_b = jnp.concatenate(y_b, axis=-1).astype(BF16)

    y = _dot(y_a, wout_ref[:pool_width, :]) + _dot(y_b, wout_ref[pool_width:, :])
    o_ref[...] = _post(x, y, mod, gpost_ref[k:k + 1, :], 1.0)


def _mix0(x, mod, norm_pre, norm_post, sub, w_in, pool_w, pool_scale, ln_g, ln_b, sgu_w, sgu_b, w_out):
    bsz, s, d = x.shape
    pool_width = pool_scale.shape[0]
    sgu_width = ln_g.shape[0]
    hd = sgu_width // SGU_HEADS
    tm = min(ROW_TILE, s)
    causal = jnp.tril(jnp.ones((SGU_CHUNK, SGU_CHUNK), dtype=bool))
    sgu_w_causal = jnp.where(causal[None], sgu_w, 0.0).astype(BF16)
    sgu_b_rows = jnp.broadcast_to(sgu_b[:, :, None], (SGU_HEADS, SGU_CHUNK, hd))
    return pl.pallas_call(
        functools.partial(_mix0_kernel, k=sub[1], pool_width=pool_width),
        grid=(bsz, s // tm),
        in_specs=[_x_spec(tm, d), _mod_spec(sub, mod), _norm_spec(sub, d), _norm_spec(sub, d),
                  _const_spec(w_in.shape), _const_spec(pool_w.shape), _const_spec((1, pool_width)),
                  _const_spec((1, sgu_width)), _const_spec((1, sgu_width)),
                  _const_spec(sgu_w.shape), _const_spec(sgu_b_rows.shape), _const_spec(w_out.shape)],
        out_specs=_x_spec(tm, d),
        out_shape=jax.ShapeDtypeStruct(x.shape, F32),
        scratch_shapes=[pltpu.VMEM((len(POOL_WINDOWS), 8, pool_width), F32)],
        compiler_params=_params("arbitrary", "arbitrary"),
    )(x, mod, norm_pre, norm_post, w_in, pool_w.astype(BF16),
      pool_scale.reshape(1, pool_width), ln_g.reshape(1, sgu_width), ln_b.reshape(1, sgu_width),
      sgu_w_causal, sgu_b_rows, w_out)


def _s5_in_kernel(x_ref, mod_ref, gpre_ref, w_ref, o_ref, u_ref, *, k):
    n_slab, rows, _ = o_ref.shape
    h = _pre(x_ref[...], _mod_rows(mod_ref), gpre_ref[k:k + 1, :]).astype(BF16)
    u = _dot(h, w_ref[...])
    for j in range(n_slab):
        u_ref[j] = u[:, j * LANES:(j + 1) * LANES]
    for l in range(SSM_CHUNK):
        for j in range(n_slab):
            o_ref[j, :, l * LANES:(l + 1) * LANES] = (
                u_ref[j, pl.ds(l, rows, stride=SSM_CHUNK), :].astype(BF16))


def _s5_in(x, mod, norm_pre, sub, w_in):
    bsz, s, d = x.shape
    width = w_in.shape[1]
    n_slab = width // LANES
    tm = min(S5_ROW_TILE, s)
    rows = tm // SSM_CHUNK
    tiles = s // tm
    return pl.pallas_call(
        functools.partial(_s5_in_kernel, k=sub[1]),
        grid=(bsz, tiles),
        in_specs=[_x_spec(tm, d), _mod_spec(sub, mod), _norm_spec(sub, d), _const_spec(w_in.shape)],
        out_specs=pl.BlockSpec((n_slab, rows, SSM_CHUNK * LANES), lambda b, i: (0, b * tiles + i, 0)),
        out_shape=jax.ShapeDtypeStruct((n_slab, bsz * s // SSM_CHUNK, SSM_CHUNK * LANES), BF16),
        scratch_shapes=[pltpu.VMEM((n_slab, tm, LANES), F32)],
        compiler_params=_params("arbitrary", "arbitrary"),
    )(x, mod, norm_pre, w_in)


def _s5_core_kernel(u_ref, lam_ref, bt_ref, ct_ref, d_ref, o_ref, st_ref, p_ref, plo_ref, qt_ref, toep_ref,
                    *, bsz, n_chunk):
    n_lt = st_ref.shape[0]
    half = n_lt // 2
    n_st = half * LANES
    u = u_ref[...]

    lam_re, lam_im = lam_ref[0:1], lam_ref[1:2]
    dt = jnp.exp(lam_ref[2:3])
    tau = lax.broadcasted_iota(jnp.int32, (SSM_CHUNK + 8, 1), 0).astype(F32)
    mag = jnp.exp(lam_re * dt * tau)
    ang = lam_im * dt * tau
    ap_re, ap_im = mag * jnp.cos(ang), mag * jnp.sin(ang)
    num_re, num_im = ap_re[1:2] - 1.0, ap_im[1:2]
    den = lam_re * lam_re + lam_im * lam_im
    cf_re = (num_re * lam_re + num_im * lam_im) / den
    cf_im = (num_im * lam_re - num_re * lam_im) / den

    n_state = n_st // SLAB_GROUPS
    same_group = (lax.broadcasted_iota(jnp.int32, (LANES, n_st), 0) // SSM_GROUP
                  == lax.broadcasted_iota(jnp.int32, (LANES, n_st), 1) // n_state)

    def block_diag(w):
        return jnp.where(same_group, jnp.concatenate([w] * (n_st // w.shape[1]), axis=-1), 0.0)

    bt_re, bt_im = block_diag(bt_ref[0]), block_diag(bt_ref[1])
    bb_re, bb_im = cf_re * bt_re - cf_im * bt_im, cf_re * bt_im + cf_im * bt_re
    ct_re, ct_im = block_diag(ct_ref[0]), block_diag(ct_ref[1])

    nt = (((1,), (1,)), ((), ()))
    for l in range(SSM_CHUNK):
        blk = slice(l * LANES, (l + 1) * LANES)
        a_r, a_i = ap_re[SSM_CHUNK - 1 - l:SSM_CHUNK - l], ap_im[SSM_CHUNK - 1 - l:SSM_CHUNK - l]
        for part, cols in ((a_r * bb_re - a_i * bb_im, slice(0, n_st)),
                           (a_r * bb_im + a_i * bb_re, slice(n_st, 2 * n_st))):
            hi = part.astype(BF16)
            p_ref[blk, cols] = hi
            plo_ref[blk, cols] = (part - hi.astype(F32)).astype(BF16)

    c_nt = jnp.concatenate([ct_re, -ct_im], axis=-1)
    c_hi = c_nt.astype(BF16)
    c_lo = (c_nt - c_hi.astype(F32)).astype(BF16)
    imp_hi = lax.dot_general(p_ref[...], jnp.concatenate([c_hi, c_lo], axis=0), nt, preferred_element_type=F32)
    imp = (imp_hi[:, :LANES] + imp_hi[:, LANES:]
           + lax.dot_general(plo_ref[...], c_hi, nt, preferred_element_type=F32))
    lag0 = (SSM_CHUNK - 1) * LANES
    toep_ref[:, LANES:] = imp.astype(BF16)
    toep_ref[:lag0, :LANES] = imp[LANES:].astype(BF16)
    toep_ref[lag0:, :LANES] = jnp.zeros((LANES, LANES), BF16)

    seq_rows = st_ref.shape[1] // bsz
    z = _dot(u, p_ref[...])
    for k in range(n_lt):
        for b in range(bsz):
            st_ref[k, b * seq_rows:b * seq_rows + n_chunk, :] = (
                z[b * n_chunk:(b + 1) * n_chunk, k * LANES:(k + 1) * LANES])

    n_tile = u.shape[1] // MXU_DIM
    col_tiles = [slice(n * MXU_DIM, (n + 1) * MXU_DIM) for n in range(n_tile)]
    d_row = jnp.concatenate([d_ref[...]] * (MXU_DIM // LANES), axis=-1)
    y_local = [_dot(u[:, :(n + 1) * MXU_DIM], toep_ref[(n_tile - 1 - n) * MXU_DIM:, :])
               + d_row * u[:, cols].astype(F32) for n, cols in enumerate(col_tiles)]

    for l in range(SSM_CHUNK):
        blk = slice(l * LANES, (l + 1) * LANES)
        a_r, a_i = ap_re[l + 1:l + 2], ap_im[l + 1:l + 2]
        qt_ref[blk, :n_st] = (a_r * ct_re - a_i * ct_im).astype(BF16)
        qt_ref[blk, n_st:] = (-(a_r * ct_im + a_i * ct_re)).astype(BF16)

    a_re = [ap_re[SSM_CHUNK:SSM_CHUNK + 1, k * LANES:(k + 1) * LANES] for k in range(half)]
    a_im = [ap_im[SSM_CHUNK:SSM_CHUNK + 1, k * LANES:(k + 1) * LANES] for k in range(half)]
    s_re = [jnp.zeros((bsz, LANES), F32)] * half
    s_im = [jnp.zeros((bsz, LANES), F32)] * half
    for c in range(n_chunk):
        rows = pl.ds(c, bsz, stride=seq_rows)
        for k in range(half):
            z_re = st_ref[k, rows, :]
            z_im = st_ref[half + k, rows, :]
            st_ref[k, rows, :] = s_re[k]
            st_ref[half + k, rows, :] = s_im[k]
            s_re[k], s_im[k] = (a_re[k] * s_re[k] - a_im[k] * s_im[k] + z_re,
                                a_re[k] * s_im[k] + a_im[k] * s_re[k] + z_im)
    st = jnp.concatenate(
        [jnp.concatenate([st_ref[k, b * seq_rows:b * seq_rows + n_chunk, :] for b in range(bsz)], axis=0)
         .astype(BF16) for k in range(n_lt)], axis=-1)

    for n, cols in enumerate(col_tiles):
        y = y_local[n] + lax.dot_general(st, qt_ref[cols, :], nt, preferred_element_type=F32)
        o_ref[:, cols] = jax.nn.gelu(y).astype(BF16)


def _s5_core(u_cm, lam_rows, bt, ct, d_cm, bsz):
    n_slab, rows, kdim = u_cm.shape
    n_st = lam_rows.shape[2]
    slab = lambda shape: pl.BlockSpec((None,) + shape, lambda j: (j,) + (0,) * len(shape))
    return pl.pallas_call(
        functools.partial(_s5_core_kernel, bsz=bsz, n_chunk=rows // bsz),
        grid=(n_slab,),
        in_specs=[slab((rows, kdim)), slab(lam_rows.shape[1:]), slab(bt.shape[1:]), slab(ct.shape[1:]),
                  slab(d_cm.shape[1:])],
        out_specs=slab((rows, kdim)),
        out_shape=jax.ShapeDtypeStruct(u_cm.shape, BF16),
        scratch_shapes=[pltpu.VMEM((2 * n_st // LANES, rows + SUBLANES * bsz, LANES), F32),
                        pltpu.VMEM((kdim, 2 * n_st), BF16),
                        pltpu.VMEM((kdim, 2 * n_st), BF16),
                        pltpu.VMEM((kdim, 2 * n_st), BF16),
                        pltpu.VMEM((kdim, 2 * LANES), BF16)],
        compiler_params=_params("arbitrary"),
    )(u_cm, lam_rows, bt, ct, d_cm)


def _s5_out_kernel(x_ref, mod_ref, gpost_ref, g_ref, w_ref, o_ref, g_nat_ref, *, k):
    n_slab, rows, _ = g_ref.shape
    for l in range(SSM_CHUNK):
        for j in range(n_slab):
            g_nat_ref[j, pl.ds(l, rows, stride=SSM_CHUNK), :] = (
                g_ref[j, :, l * LANES:(l + 1) * LANES].astype(F32))
    g = jnp.concatenate([g_nat_ref[j].astype(BF16) for j in range(n_slab)], axis=-1)
    ab = _dot(g, w_ref[...])
    d = ab.shape[1] // 2
    y = ab[:, :d] * _sigmoid(ab[:, d:])
    o_ref[...] = _post(x_ref[...], y, _mod_rows(mod_ref), gpost_ref[k:k + 1, :], 1.0)


def _s5_out(x, mod, norm_post, sub, g_cm, w_glu):
    bsz, s, d = x.shape
    n_slab = g_cm.shape[0]
    tm = min(S5_ROW_TILE, s)
    rows = tm // SSM_CHUNK
    tiles = s // tm
    return pl.pallas_call(
        functools.partial(_s5_out_kernel, k=sub[1]),
        grid=(bsz, tiles),
        in_specs=[_x_spec(tm, d), _mod_spec(sub, mod), _norm_spec(sub, d),
                  pl.BlockSpec((n_slab, rows, SSM_CHUNK * LANES), lambda b, i: (0, b * tiles + i, 0)),
                  _const_spec(w_glu.shape)],
        out_specs=_x_spec(tm, d),
        out_shape=jax.ShapeDtypeStruct(x.shape, F32),
        scratch_shapes=[pltpu.VMEM((n_slab, tm, LANES), F32)],
        compiler_params=_params("arbitrary", "arbitrary"),
    )(x, mod, norm_post, g_cm, w_glu)


def _s5_slab_params(lam_re, lam_im, b_re, b_im, c_re, c_im, d_skip, log_dt):
    n_group, n_state = lam_re.shape
    n_slab = n_group // SLAB_GROUPS
    n_st = SLAB_GROUPS * n_state
    log_dt_st = jnp.broadcast_to(log_dt[:, None], (n_group, n_state))
    lam_rows = jnp.stack([v.reshape(n_slab, n_st) for v in (lam_re, lam_im, log_dt_st)], axis=1)

    def rows_by_channel(w_re, w_im):
        w = jnp.stack([w_re, w_im], axis=0).reshape(2, n_slab, LANES, n_state).transpose(1, 0, 2, 3)
        return jnp.concatenate([w] * (LANES // n_state), axis=-1)

    bt = rows_by_channel(b_re.transpose(0, 2, 1), b_im.transpose(0, 2, 1))
    ct = rows_by_channel(c_re, c_im)
    return lam_rows, bt, ct, d_skip.reshape(n_slab, 1, LANES)


def kernel(x, c, ada_w, ada_b, norm_pre, norm_post, ffn_w_in, ffn_w_out, ab_w_in, pool_w, pool_scale, sgu_ln_g, sgu_ln_b, sgu_w, sgu_b, ab_w_out, ssm_w_in, ssm_lam_re, ssm_lam_im, ssm_b_re, ssm_b_im, ssm_c_re, ssm_c_im, ssm_d, ssm_log_dt, ssm_w_glu):
    bsz, s, d = x.shape
    depth = ada_w.shape[0]
    assert s % FFN_ROW_TILE == 0 or s < ROW_TILE
    mod = _ada_mod(c, ada_w, ada_b)

    def ffn_casts(l, half):
        return [(ffn_w_in, (l, half)), (ffn_w_out, (l, half))]

    def mixer_casts(l):
        i = l // 2
        return [(ab_w_in, (i,)), (ab_w_out, (i,))] if l % 2 == 0 else [(ssm_w_in, (i,)), (ssm_w_glu, (i,))]

    w_ffn = [ffn_w_in[0, 0].astype(BF16), ffn_w_out[0, 0].astype(BF16)]
    for l in range(depth):
        i = l // 2
        x, cast = _ffn(x, mod, norm_pre, norm_post, (l, 0), *w_ffn, casts=mixer_casts(l) + ffn_casts(l, 1))
        w_mix, w_ffn = cast[:2], cast[2:]
        if l % 2 == 0:
            x = _mix0(x, mod, norm_pre, norm_post, (l, 1), w_mix[0], pool_w[i], pool_scale[i],
                      sgu_ln_g[i], sgu_ln_b[i], sgu_w[i], sgu_b[i], w_mix[1])
        else:
            slab_params = _s5_slab_params(ssm_lam_re[i], ssm_lam_im[i], ssm_b_re[i], ssm_b_im[i],
                                          ssm_c_re[i], ssm_c_im[i], ssm_d[i], ssm_log_dt[i])
            u_cm = _s5_in(x, mod, norm_pre, (l, 1), w_mix[0])
            g_cm = _s5_core(u_cm, *slab_params, bsz)
            x = _s5_out(x, mod, norm_post, (l, 1), g_cm, w_mix[1])
        x, w_ffn = _ffn(x, mod, norm_pre, norm_post, (l, 2), *w_ffn,
                        casts=ffn_casts(l + 1, 0) if l + 1 < depth else [])
    return x
```

```python
import functools

import jax
import jax.numpy as jnp
from jax import lax
from jax.experimental import pallas as pl
from jax.experimental.pallas import tpu as pltpu

F32 = jnp.float32
BF16 = jnp.bfloat16
EPS = 1e-6
N_SUB = 3
FFN_RES_WEIGHT = 0.5
POOL_WINDOWS = (2, 4, 8, 16)
SGU_HEADS = 4
SGU_CHUNK = 128
SSM_GROUP = 16
SSM_CHUNK = 8
LANES = 128
SUBLANES = 8
SLAB_GROUPS = LANES // SSM_GROUP
MXU_DIM = 256
VMEM_LIMIT_BYTES = 56 * 1024 * 1024
ROW_TILE = 1024
MIX_SPLIT = 2
S5_ROW_TILE = 1024
FFN_ROW_TILE = 1024
FFN_SPLIT = 2


def _const_spec(shape, lead=()):
    nd = len(shape)
    return pl.BlockSpec((None,) * len(lead) + tuple(shape), lambda *_: tuple(lead) + (0,) * nd,
                        pipeline_mode=pl.Buffered(1))


def _params(*sem):
    return pltpu.CompilerParams(dimension_semantics=sem, vmem_limit_bytes=VMEM_LIMIT_BYTES)


def _rms(x, g):
    return x * lax.rsqrt(jnp.mean(x * x, axis=-1, keepdims=True) + EPS) * g


def _mod_rows(mod_ref):
    b = pl.program_id(0)
    return [mod_ref[r, pl.ds(b, 1), :] for r in range(3)]


def _pre(x, mod, g_pre):
    return _rms(x, g_pre * (1.0 + mod[1])) + mod[0]


def _post(x, y, mod, g_post, res_weight):
    return x + _rms(y, res_weight * mod[2] * g_post)


def _sigmoid(v):
    return 1.0 / (1.0 + jnp.exp(-v))


def _dot(a, b):
    return jnp.dot(a, b, preferred_element_type=F32)


def _ada_kernel(c_ref, w_ref, b_ref, o_ref):
    c = c_ref[...]
    cond = (c * _sigmoid(c)).astype(BF16)
    o_ref[...] = _dot(cond, w_ref[...].astype(BF16)) + b_ref[...]


def _ada_mod(c, ada_w, ada_b):
    depth, d, n = ada_w.shape
    bsz = c.shape[0]
    out = pl.pallas_call(
        _ada_kernel,
        grid=(depth, n // d),
        in_specs=[
            pl.BlockSpec((bsz, d), lambda l, j: (0, 0)),
            pl.BlockSpec((None, d, d), lambda l, j: (l, 0, j)),
            pl.BlockSpec((None, 1, d), lambda l, j: (l, 0, j)),
        ],
        out_specs=pl.BlockSpec((None, None, bsz, d), lambda l, j: (l, j, 0, 0)),
        out_shape=jax.ShapeDtypeStruct((depth, n // d, bsz, d), F32),
        compiler_params=_params("arbitrary", "arbitrary"),
    )(c, ada_w, ada_b.reshape(depth, 1, n))
    return out.reshape(depth, N_SUB, 3, bsz, d)


def _x_spec(tm, d):
    return pl.BlockSpec((None, tm, d), lambda b, i: (b, i, 0))


def _mod_spec(sub, mod):
    return _const_spec(mod.shape[2:], lead=sub)


def _norm_spec(sub, d):
    return _const_spec((N_SUB, d), lead=sub[:1])


def _ffn_kernel(x_ref, mod_ref, gpre_ref, gpost_ref, win_ref, wout_ref, *rest, k, d_ff, fc, n_split):
    n_cast = len(rest) // 2
    o_ref = rest[n_cast]
    for src_ref, dst_ref in zip(rest[:n_cast], rest[n_cast + 1:]):
        dst_ref[...] = src_ref[...].astype(BF16)

    mod = _mod_rows(mod_ref)
    sub = x_ref.shape[0] // n_split
    rows = [slice(p * sub, (p + 1) * sub) for p in range(n_split)]
    h = [_pre(x_ref[r, :], mod, gpre_ref[k:k + 1, :]).astype(BF16) for r in rows]
    acc = [jnp.zeros((sub, x_ref.shape[1]), F32)] * n_split
    for j in range(d_ff // fc):
        for p in range(n_split):
            a = _dot(h[p], win_ref[:, j * fc:(j + 1) * fc])
            b = _dot(h[p], win_ref[:, d_ff + j * fc:d_ff + (j + 1) * fc])
            act = (a * _sigmoid(a) * b).astype(BF16)
            acc[p] = acc[p] + _dot(act, wout_ref[j * fc:(j + 1) * fc, :])
    for p, r in enumerate(rows):
        o_ref[r, :] = _post(x_ref[r, :], acc[p], mod, gpost_ref[k:k + 1, :], FFN_RES_WEIGHT)


def _ffn(x, mod, norm_pre, norm_post, sub, w_in, w_out, casts=()):
    bsz, s, d = x.shape
    d_ff = w_out.shape[0]
    tm = min(FFN_ROW_TILE, s)
    tiles = s // tm
    steps = bsz * tiles
    cast_in, cast_out, cast_shapes = [], [], []
    for w, lead in casts:
        r, c = w.shape[len(lead):]
        cast_in.append(pl.BlockSpec((None,) * len(lead) + (r // steps, c),
                                    lambda b, i, lead=lead: lead + (b * tiles + i, 0)))
        cast_out.append(pl.BlockSpec((r // steps, c), lambda b, i: (b * tiles + i, 0)))
        cast_shapes.append(jax.ShapeDtypeStruct((r, c), BF16))
    out = pl.pallas_call(
        functools.partial(_ffn_kernel, k=sub[1], d_ff=d_ff, fc=MXU_DIM, n_split=FFN_SPLIT),
        grid=(bsz, tiles),
        in_specs=[_x_spec(tm, d), _mod_spec(sub, mod), _norm_spec(sub, d), _norm_spec(sub, d),
                  _const_spec(w_in.shape), _const_spec(w_out.shape)] + cast_in,
        out_specs=[_x_spec(tm, d)] + cast_out,
        out_shape=[jax.ShapeDtypeStruct(x.shape, F32)] + cast_shapes,
        compiler_params=_params("arbitrary", "arbitrary"),
    )(x, mod, norm_pre, norm_post, w_in, w_out, *[w for w, _ in casts])
    return out[0], list(out[1:])


def _mix0_kernel(x_ref, mod_ref, gpre_ref, gpost_ref, win_ref, poolw_ref, pscale_ref, lng_ref, lnb_ref,
                 sguw_ref, sgub_ref, wout_ref, o_ref, carry_ref, *, k, pool_width, n_split):
    i = pl.program_id(1)
    tm = x_ref.shape[0]
    sub = tm // n_split
    n_pool = len(POOL_WINDOWS)
    gd = pool_width // n_pool

    @pl.when(i == 0)
    def _():
        carry_ref[...] = jnp.zeros(carry_ref.shape, F32)

    mod = _mod_rows(mod_ref)
    row8 = lax.broadcasted_iota(jnp.int32, (8, 1), 0)

    def shifted(cur, level, dist):
        lanes = cur.shape[1]
        prev8 = carry_ref[level, :, pool_width - lanes:]
        carry_ref[level, :, pool_width - lanes:] = cur[sub - 8:, :]
        rolled = pltpu.roll(cur, dist, axis=0)
        if dist == 8:
            return jnp.concatenate([prev8, rolled[8:]], axis=0)
        head = jnp.where(row8 < dist, pltpu.roll(prev8, dist, axis=0), rolled[:8])
        return jnp.concatenate([head, rolled[8:]], axis=0)

    def project(st, rows):
        st["x"] = x_ref[rows, :]
        h = _pre(st["x"], mod, gpre_ref[k:k + 1, :]).astype(BF16)
        st["z"] = _dot(h, win_ref[...])

    def pool(st, rows):
        a = st["z"][:, :pool_width]
        pos = (i * tm + rows.start + 1 + lax.broadcasted_iota(jnp.int32, (sub, 1), 0)).astype(F32)
        level_sum = a
        y_a = []
        for g, w in enumerate(POOL_WINDOWS):
            level_sum = level_sum + shifted(level_sum, g, w // 2)
            mean = level_sum[:, :gd] / jnp.minimum(pos, float(w))
            dg = (mean - a[:, g * gd:(g + 1) * gd]).astype(BF16)
            y_a.append(_dot(dg, poolw_ref[g]) * pscale_ref[:, g * gd:(g + 1) * gd])
            level_sum = level_sum[:, gd:]
        st["y_a"] = jnp.concatenate(y_a, axis=-1).astype(BF16)

    def gate(st, rows):
        zb = jax.nn.gelu(st["z"][:, pool_width:])
        sgu_width = zb.shape[1] // 2
        hd = sgu_width // SGU_HEADS
        u = zb[:, :sgu_width]
        v = zb[:, sgu_width:]
        y_b = []
        for hh in range(SGU_HEADS):
            cols = slice(hh * hd, (hh + 1) * hd)
            vh = v[:, cols]
            mu = jnp.mean(vh, axis=-1, keepdims=True)
            var = jnp.mean(jnp.square(vh - mu), axis=-1, keepdims=True)
            vn = ((vh - mu) * lax.rsqrt(var + EPS) * lng_ref[:, cols] + lnb_ref[:, cols]).astype(BF16)
            s_h = [_dot(sguw_ref[hh], vn[ck * SGU_CHUNK:(ck + 1) * SGU_CHUNK]) + sgub_ref[hh]
                   for ck in range(sub // SGU_CHUNK)]
            y_b.append(u[:, cols] * jnp.concatenate(s_h, axis=0))
        st["y_b"] = jnp.concatenate(y_b, axis=-1).astype(BF16)

    def finish(st, rows):
        y = _dot(st["y_a"], wout_ref[:pool_width, :]) + _dot(st["y_b"], wout_ref[pool_width:, :])
        o_ref[rows, :] = _post(st["x"], y, mod, gpost_ref[k:k + 1, :], 1.0)

    stages = (project, pool, gate, finish)
    state = [{} for _ in range(n_split)]
    for t in range(len(stages) + n_split - 1):
        for p in range(n_split):
            if 0 <= t - p < len(stages):
                stages[t - p](state[p], slice(p * sub, (p + 1) * sub))


def _mix0(x, mod, norm_pre, norm_post, sub, w_in, pool_w, pool_scale, ln_g, ln_b, sgu_w, sgu_b, w_out):
    bsz, s, d = x.shape
    pool_width = pool_scale.shape[0]
    sgu_width = ln_g.shape[0]
    hd = sgu_width // SGU_HEADS
    tm = min(ROW_TILE, s)
    causal = jnp.tril(jnp.ones((SGU_CHUNK, SGU_CHUNK), dtype=bool))
    sgu_w_causal = jnp.where(causal[None], sgu_w, 0.0).astype(BF16)
    sgu_b_rows = jnp.broadcast_to(sgu_b[:, :, None], (SGU_HEADS, SGU_CHUNK, hd))
    return pl.pallas_call(
        functools.partial(_mix0_kernel, k=sub[1], pool_width=pool_width, n_split=MIX_SPLIT),
        grid=(bsz, s // tm),
        in_specs=[_x_spec(tm, d), _mod_spec(sub, mod), _norm_spec(sub, d), _norm_spec(sub, d),
                  _const_spec(w_in.shape), _const_spec(pool_w.shape), _const_spec((1, pool_width)),
                  _const_spec((1, sgu_width)), _const_spec((1, sgu_width)),
                  _const_spec(sgu_w.shape), _const_spec(sgu_b_rows.shape), _const_spec(w_out.shape)],
        out_specs=_x_spec(tm, d),
        out_shape=jax.ShapeDtypeStruct(x.shape, F32),
        scratch_shapes=[pltpu.VMEM((len(POOL_WINDOWS), 8, pool_width), F32)],
        compiler_params=_params("arbitrary", "arbitrary"),
    )(x, mod, norm_pre, norm_post, w_in, pool_w.astype(BF16),
      pool_scale.reshape(1, pool_width), ln_g.reshape(1, sgu_width), ln_b.reshape(1, sgu_width),
      sgu_w_causal, sgu_b_rows, w_out)


def _s5_in_kernel(x_ref, mod_ref, gpre_ref, w_ref, o_ref, u_ref, *, k):
    n_slab, rows, _ = o_ref.shape
    h = _pre(x_ref[...], _mod_rows(mod_ref), gpre_ref[k:k + 1, :]).astype(BF16)
    u = _dot(h, w_ref[...])
    for j in range(n_slab):
        u_ref[j] = u[:, j * LANES:(j + 1) * LANES]
    for l in range(SSM_CHUNK):
        for j in range(n_slab):
            o_ref[j, :, l * LANES:(l + 1) * LANES] = (
                u_ref[j, pl.ds(l, rows, stride=SSM_CHUNK), :].astype(BF16))


def _s5_in(x, mod, norm_pre, sub, w_in):
    bsz, s, d = x.shape
    width = w_in.shape[1]
    n_slab = width // LANES
    tm = min(S5_ROW_TILE, s)
    rows = tm // SSM_CHUNK
    tiles = s // tm
    return pl.pallas_call(
        functools.partial(_s5_in_kernel, k=sub[1]),
        grid=(bsz, tiles),
        in_specs=[_x_spec(tm, d), _mod_spec(sub, mod), _norm_spec(sub, d), _const_spec(w_in.shape)],
        out_specs=pl.BlockSpec((n_slab, rows, SSM_CHUNK * LANES), lambda b, i: (0, b * tiles + i, 0)),
        out_shape=jax.ShapeDtypeStruct((n_slab, bsz * s // SSM_CHUNK, SSM_CHUNK * LANES), BF16),
        scratch_shapes=[pltpu.VMEM((n_slab, tm, LANES), F32)],
        compiler_params=_params("arbitrary", "arbitrary"),
    )(x, mod, norm_pre, w_in)


def _s5_core_kernel(u_ref, lam_ref, bt_ref, ct_ref, d_ref, o_ref, st_ref, p_ref, plo_ref, qt_ref, toep_ref,
                    *, bsz, n_chunk):
    n_lt = st_ref.shape[0]
    half = n_lt // 2
    n_st = half * LANES
    u = u_ref[...]

    lam_re, lam_im = lam_ref[0:1], lam_ref[1:2]
    dt = jnp.exp(lam_ref[2:3])
    tau = lax.broadcasted_iota(jnp.int32, (SSM_CHUNK + 8, 1), 0).astype(F32)
    mag = jnp.exp(lam_re * dt * tau)
    ang = lam_im * dt * tau
    ap_re, ap_im = mag * jnp.cos(ang), mag * jnp.sin(ang)
    num_re, num_im = ap_re[1:2] - 1.0, ap_im[1:2]
    den = lam_re * lam_re + lam_im * lam_im
    cf_re = (num_re * lam_re + num_im * lam_im) / den
    cf_im = (num_im * lam_re - num_re * lam_im) / den

    n_state = n_st // SLAB_GROUPS
    same_group = (lax.broadcasted_iota(jnp.int32, (LANES, n_st), 0) // SSM_GROUP
                  == lax.broadcasted_iota(jnp.int32, (LANES, n_st), 1) // n_state)

    def block_diag(w):
        return jnp.where(same_group, jnp.concatenate([w] * (n_st // w.shape[1]), axis=-1), 0.0)

    bt_re, bt_im = block_diag(bt_ref[0]), block_diag(bt_ref[1])
    bb_re, bb_im = cf_re * bt_re - cf_im * bt_im, cf_re * bt_im + cf_im * bt_re
    ct_re, ct_im = block_diag(ct_ref[0]), block_diag(ct_ref[1])

    nt = (((1,), (1,)), ((), ()))
    for l in range(SSM_CHUNK):
        blk = slice(l * LANES, (l + 1) * LANES)
        a_r, a_i = ap_re[SSM_CHUNK - 1 - l:SSM_CHUNK - l], ap_im[SSM_CHUNK - 1 - l:SSM_CHUNK - l]
        for part, cols in ((a_r * bb_re - a_i * bb_im, slice(0, n_st)),
                           (a_r * bb_im + a_i * bb_re, slice(n_st, 2 * n_st))):
            hi = part.astype(BF16)
            p_ref[blk, cols] = hi
            plo_ref[blk, cols] = (part - hi.astype(F32)).astype(BF16)

    c_nt = jnp.concatenate([ct_re, -ct_im], axis=-1)
    c_hi = c_nt.astype(BF16)
    c_lo = (c_nt - c_hi.astype(F32)).astype(BF16)
    imp_hi = lax.dot_general(p_ref[...], jnp.concatenate([c_hi, c_lo], axis=0), nt, preferred_element_type=F32)
    imp = (imp_hi[:, :LANES] + imp_hi[:, LANES:]
           + lax.dot_general(plo_ref[...], c_hi, nt, preferred_element_type=F32))
    lag0 = (SSM_CHUNK - 1) * LANES
    toep_ref[:, LANES:] = imp.astype(BF16)
    toep_ref[:lag0, :LANES] = imp[LANES:].astype(BF16)
    toep_ref[lag0:, :LANES] = jnp.zeros((LANES, LANES), BF16)

    seq_rows = st_ref.shape[1] // bsz
    z = _dot(u, p_ref[...])
    for k in range(n_lt):
        for b in range(bsz):
            st_ref[k, b * seq_rows:b * seq_rows + n_chunk, :] = (
                z[b * n_chunk:(b + 1) * n_chunk, k * LANES:(k + 1) * LANES])

    n_tile = u.shape[1] // MXU_DIM
    col_tiles = [slice(n * MXU_DIM, (n + 1) * MXU_DIM) for n in range(n_tile)]
    d_row = jnp.concatenate([d_ref[...]] * (MXU_DIM // LANES), axis=-1)
    y_local = [_dot(u[:, :(n + 1) * MXU_DIM], toep_ref[(n_tile - 1 - n) * MXU_DIM:, :])
               + d_row * u[:, cols].astype(F32) for n, cols in enumerate(col_tiles)]

    for l in range(SSM_CHUNK):
        blk = slice(l * LANES, (l + 1) * LANES)
        a_r, a_i = ap_re[l + 1:l + 2], ap_im[l + 1:l + 2]
        qt_ref[blk, :n_st] = (a_r * ct_re - a_i * ct_im).astype(BF16)
        qt_ref[blk, n_st:] = (-(a_r * ct_im + a_i * ct_re)).astype(BF16)

    a_re = [ap_re[SSM_CHUNK:SSM_CHUNK + 1, k * LANES:(k + 1) * LANES] for k in range(half)]
    a_im = [ap_im[SSM_CHUNK:SSM_CHUNK + 1, k * LANES:(k + 1) * LANES] for k in range(half)]
    s_re = [jnp.zeros((bsz, LANES), F32)] * half
    s_im = [jnp.zeros((bsz, LANES), F32)] * half
    for c in range(n_chunk):
        rows = pl.ds(c, bsz, stride=seq_rows)
        for k in range(half):
            z_re = st_ref[k, rows, :]
            z_im = st_ref[half + k, rows, :]
            st_ref[k, rows, :] = s_re[k]
            st_ref[half + k, rows, :] = s_im[k]
            s_re[k], s_im[k] = (a_re[k] * s_re[k] - a_im[k] * s_im[k] + z_re,
                                a_re[k] * s_im[k] + a_im[k] * s_re[k] + z_im)
    st = jnp.concatenate(
        [jnp.concatenate([st_ref[k, b * seq_rows:b * seq_rows + n_chunk, :] for b in range(bsz)], axis=0)
         .astype(BF16) for k in range(n_lt)], axis=-1)

    for n, cols in enumerate(col_tiles):
        y = y_local[n] + lax.dot_general(st, qt_ref[cols, :], nt, preferred_element_type=F32)
        o_ref[:, cols] = jax.nn.gelu(y).astype(BF16)


def _s5_core(u_cm, lam_rows, bt, ct, d_cm, bsz):
    n_slab, rows, kdim = u_cm.shape
    n_st = lam_rows.shape[2]
    slab = lambda shape: pl.BlockSpec((None,) + shape, lambda j: (j,) + (0,) * len(shape))
    return pl.pallas_call(
        functools.partial(_s5_core_kernel, bsz=bsz, n_chunk=rows // bsz),
        grid=(n_slab,),
        in_specs=[slab((rows, kdim)), slab(lam_rows.shape[1:]), slab(bt.shape[1:]), slab(ct.shape[1:]),
                  slab(d_cm.shape[1:])],
        out_specs=slab((rows, kdim)),
        out_shape=jax.ShapeDtypeStruct(u_cm.shape, BF16),
        scratch_shapes=[pltpu.VMEM((2 * n_st // LANES, rows + SUBLANES * bsz, LANES), F32),
                        pltpu.VMEM((kdim, 2 * n_st), BF16),
                        pltpu.VMEM((kdim, 2 * n_st), BF16),
                        pltpu.VMEM((kdim, 2 * n_st), BF16),
                        pltpu.VMEM((kdim, 2 * LANES), BF16)],
        compiler_params=_params("arbitrary"),
    )(u_cm, lam_rows, bt, ct, d_cm)


def _s5_out_kernel(x_ref, mod_ref, gpost_ref, g_ref, w_ref, o_ref, g_nat_ref, *, k):
    n_slab, rows, _ = g_ref.shape
    for l in range(SSM_CHUNK):
        for j in range(n_slab):
            g_nat_ref[j, pl.ds(l, rows, stride=SSM_CHUNK), :] = (
                g_ref[j, :, l * LANES:(l + 1) * LANES].astype(F32))
    g = jnp.concatenate([g_nat_ref[j].astype(BF16) for j in range(n_slab)], axis=-1)
    ab = _dot(g, w_ref[...])
    d = ab.shape[1] // 2
    y = ab[:, :d] * _sigmoid(ab[:, d:])
    o_ref[...] = _post(x_ref[...], y, _mod_rows(mod_ref), gpost_ref[k:k + 1, :], 1.0)


def _s5_out(x, mod, norm_post, sub, g_cm, w_glu):
    bsz, s, d = x.shape
    n_slab = g_cm.shape[0]
    tm = min(S5_ROW_TILE, s)
    rows = tm // SSM_CHUNK
    tiles = s // tm
    return pl.pallas_call(
        functools.partial(_s5_out_kernel, k=sub[1]),
        grid=(bsz, tiles),
        in_specs=[_x_spec(tm, d), _mod_spec(sub, mod), _norm_spec(sub, d),
                  pl.BlockSpec((n_slab, rows, SSM_CHUNK * LANES), lambda b, i: (0, b * tiles + i, 0)),
                  _const_spec(w_glu.shape)],
        out_specs=_x_spec(tm, d),
        out_shape=jax.ShapeDtypeStruct(x.shape, F32),
        scratch_shapes=[pltpu.VMEM((n_slab, tm, LANES), F32)],
        compiler_params=_params("arbitrary", "arbitrary"),
    )(x, mod, norm_post, g_cm, w_glu)


def _s5_slab_params(lam_re, lam_im, b_re, b_im, c_re, c_im, d_skip, log_dt):
    n_group, n_state = lam_re.shape
    n_slab = n_group // SLAB_GROUPS
    n_st = SLAB_GROUPS * n_state
    log_dt_st = jnp.broadcast_to(log_dt[:, None], (n_group, n_state))
    lam_rows = jnp.stack([v.reshape(n_slab, n_st) for v in (lam_re, lam_im, log_dt_st)], axis=1)

    def rows_by_channel(w_re, w_im):
        w = jnp.stack([w_re, w_im], axis=0).reshape(2, n_slab, LANES, n_state).transpose(1, 0, 2, 3)
        return jnp.concatenate([w] * (LANES // n_state), axis=-1)

    bt = rows_by_channel(b_re.transpose(0, 2, 1), b_im.transpose(0, 2, 1))
    ct = rows_by_channel(c_re, c_im)
    return lam_rows, bt, ct, d_skip.reshape(n_slab, 1, LANES)


def kernel(x, c, ada_w, ada_b, norm_pre, norm_post, ffn_w_in, ffn_w_out, ab_w_in, pool_w, pool_scale, sgu_ln_g, sgu_ln_b, sgu_w, sgu_b, ab_w_out, ssm_w_in, ssm_lam_re, ssm_lam_im, ssm_b_re, ssm_b_im, ssm_c_re, ssm_c_im, ssm_d, ssm_log_dt, ssm_w_glu):
    bsz, s, d = x.shape
    depth = ada_w.shape[0]
    assert all(s % min(tile, s) == 0 for tile in (ROW_TILE, S5_ROW_TILE, FFN_ROW_TILE))
    mod = _ada_mod(c, ada_w, ada_b)

    def ffn_casts(l, half):
        return [(ffn_w_in, (l, half)), (ffn_w_out, (l, half))]

    def mixer_casts(l):
        i = l // 2
        return [(ab_w_in, (i,)), (ab_w_out, (i,))] if l % 2 == 0 else [(ssm_w_in, (i,)), (ssm_w_glu, (i,))]

    w_ffn = [ffn_w_in[0, 0].astype(BF16), ffn_w_out[0, 0].astype(BF16)]
    for l in range(depth):
        i = l // 2
        x, cast = _ffn(x, mod, norm_pre, norm_post, (l, 0), *w_ffn, casts=mixer_casts(l) + ffn_casts(l, 1))
        w_mix, w_ffn = cast[:2], cast[2:]
        if l % 2 == 0:
            x = _mix0(x, mod, norm_pre, norm_post, (l, 1), w_mix[0], pool_w[i], pool_scale[i],
                      sgu_ln_g[i], sgu_ln_b[i], sgu_w[i], sgu_b[i], w_mix[1])
        else:
            slab_params = _s5_slab_params(ssm_lam_re[i], ssm_lam_im[i], ssm_b_re[i], ssm_b_im[i],
                                          ssm_c_re[i], ssm_c_im[i], ssm_d[i], ssm_log_dt[i])
            u_cm = _s5_in(x, mod, norm_pre, (l, 1), w_mix[0])
            g_cm = _s5_core(u_cm, *slab_params, bsz)
            x = _s5_out(x, mod, norm_post, (l, 1), g_cm, w_mix[1])
        x, w_ffn = _ffn(x, mod, norm_pre, norm_post, (l, 2), *w_ffn,
                        casts=ffn_casts(l + 1, 0) if l + 1 < depth else [])
    return x
```

```python
import functools

import jax
import jax.numpy as jnp
from jax import lax
from jax.experimental import pallas as pl
from jax.experimental.pallas import tpu as pltpu

F32 = jnp.float32
BF16 = jnp.bfloat16
EPS = 1e-6
N_SUB = 3
FFN_RES_WEIGHT = 0.5
POOL_WINDOWS = (2, 4, 8, 16)
SGU_HEADS = 4
SGU_CHUNK = 128
SSM_GROUP = 16
SSM_CHUNK = 8
LANES = 128
SUBLANES = 8
SLAB_GROUPS = LANES // SSM_GROUP
MXU_DIM = 256
VMEM_LIMIT_BYTES = 56 * 1024 * 1024
ROW_TILE = 1024
MIX_SPLIT = 2
S5_ROW_TILE = 1024
S5_SPLIT = 1
FFN_ROW_TILE = 1024
FFN_SPLIT = 2


def _const_spec(shape, lead=()):
    nd = len(shape)
    return pl.BlockSpec((None,) * len(lead) + tuple(shape), lambda *_: tuple(lead) + (0,) * nd,
                        pipeline_mode=pl.Buffered(1))


def _params(*sem):
    return pltpu.CompilerParams(dimension_semantics=sem, vmem_limit_bytes=VMEM_LIMIT_BYTES)


def _rms(x, g):
    return x * lax.rsqrt(jnp.mean(x * x, axis=-1, keepdims=True) + EPS) * g


def _mod_rows(mod_ref):
    b = pl.program_id(0)
    return [mod_ref[r, pl.ds(b, 1), :] for r in range(3)]


def _pre(x, mod, g_pre):
    return _rms(x, g_pre * (1.0 + mod[1])) + mod[0]


def _post(x, y, mod, g_post, res_weight):
    return x + _rms(y, res_weight * mod[2] * g_post)


def _sigmoid(v):
    return 1.0 / (1.0 + jnp.exp(-v))


def _dot(a, b):
    return jnp.dot(a, b, preferred_element_type=F32)


def _staggered(stages, n_split, sub):
    state = [{} for _ in range(n_split)]
    for t in range(len(stages) + n_split - 1):
        for p in range(n_split):
            if 0 <= t - p < len(stages):
                stages[t - p](state[p], p * sub)


def _ada_kernel(c_ref, w_ref, b_ref, o_ref):
    c = c_ref[...]
    d = c.shape[1]
    cond = (c * _sigmoid(c)).astype(BF16)
    for r in range(o_ref.shape[0]):
        cols = slice(r * d, (r + 1) * d)
        o_ref[r] = _dot(cond, w_ref[:, cols].astype(BF16)) + b_ref[:, cols]


def _ada_mod(c, ada_w, ada_b):
    depth, d, n = ada_w.shape
    bsz = c.shape[0]
    out = pl.pallas_call(
        _ada_kernel,
        grid=(depth, N_SUB),
        in_specs=[
            pl.BlockSpec((bsz, d), lambda l, j: (0, 0)),
            pl.BlockSpec((None, d, 3 * d), lambda l, j: (l, 0, j)),
            pl.BlockSpec((None, 1, 3 * d), lambda l, j: (l, 0, j)),
        ],
        out_specs=pl.BlockSpec((None, 3, bsz, d), lambda l, j: (l, j, 0, 0)),
        out_shape=jax.ShapeDtypeStruct((depth, n // d, bsz, d), F32),
        compiler_params=_params("arbitrary", "arbitrary"),
    )(c, ada_w, ada_b.reshape(depth, 1, n))
    return out.reshape(depth, N_SUB, 3, bsz, d)


def _x_spec(tm, d):
    return pl.BlockSpec((None, tm, d), lambda b, i: (b, i, 0))


def _mod_spec(sub, mod):
    return _const_spec(mod.shape[2:], lead=sub)


def _norm_spec(sub, d):
    return _const_spec((N_SUB, d), lead=sub[:1])


def _ffn_kernel(x_ref, mod_ref, gpre_ref, gpost_ref, win_ref, wout_ref, *rest, k, d_ff, fc, n_split):
    n_cast = len(rest) // 2
    o_ref = rest[n_cast]
    for src_ref, dst_ref in zip(rest[:n_cast], rest[n_cast + 1:]):
        dst_ref[...] = src_ref[...].astype(BF16)

    mod = _mod_rows(mod_ref)
    sub = x_ref.shape[0] // n_split
    rows = [slice(p * sub, (p + 1) * sub) for p in range(n_split)]
    h = [_pre(x_ref[r, :], mod, gpre_ref[k:k + 1, :]).astype(BF16) for r in rows]
    acc = [jnp.zeros((sub, x_ref.shape[1]), F32)] * n_split
    for j in range(d_ff // fc):
        for p in range(n_split):
            a = _dot(h[p], win_ref[:, j * fc:(j + 1) * fc])
            b = _dot(h[p], win_ref[:, d_ff + j * fc:d_ff + (j + 1) * fc])
            act = (a * _sigmoid(a) * b).astype(BF16)
            acc[p] = acc[p] + _dot(act, wout_ref[j * fc:(j + 1) * fc, :])
    for p, r in enumerate(rows):
        o_ref[r, :] = _post(x_ref[r, :], acc[p], mod, gpost_ref[k:k + 1, :], FFN_RES_WEIGHT)


def _ffn(x, mod, norm_pre, norm_post, sub, w_in, w_out, casts=()):
    bsz, s, d = x.shape
    d_ff = w_out.shape[0]
    tm = min(FFN_ROW_TILE, s)
    tiles = s // tm
    steps = bsz * tiles
    cast_in, cast_out, cast_shapes = [], [], []
    for w, lead in casts:
        r, c = w.shape[len(lead):]
        cast_in.append(pl.BlockSpec((None,) * len(lead) + (r // steps, c),
                                    lambda b, i, lead=lead: lead + (b * tiles + i, 0)))
        cast_out.append(pl.BlockSpec((r // steps, c), lambda b, i: (b * tiles + i, 0)))
        cast_shapes.append(jax.ShapeDtypeStruct((r, c), BF16))
    out = pl.pallas_call(
        functools.partial(_ffn_kernel, k=sub[1], d_ff=d_ff, fc=MXU_DIM, n_split=FFN_SPLIT),
        grid=(bsz, tiles),
        in_specs=[_x_spec(tm, d), _mod_spec(sub, mod), _norm_spec(sub, d), _norm_spec(sub, d),
                  _const_spec(w_in.shape), _const_spec(w_out.shape)] + cast_in,
        out_specs=[_x_spec(tm, d)] + cast_out,
        out_shape=[jax.ShapeDtypeStruct(x.shape, F32)] + cast_shapes,
        compiler_params=_params("arbitrary", "arbitrary"),
    )(x, mod, norm_pre, norm_post, w_in, w_out, *[w for w, _ in casts])
    return out[0], list(out[1:])


def _mix0_kernel(x_ref, mod_ref, gpre_ref, gpost_ref, win_ref, poolw_ref, pscale_ref, lng_ref, lnb_ref,
                 sguw_ref, sgub_ref, wout_ref, o_ref, carry_ref, *, k, pool_width, n_split):
    i = pl.program_id(1)
    tm = x_ref.shape[0]
    sub = tm // n_split
    n_pool = len(POOL_WINDOWS)
    gd = pool_width // n_pool

    @pl.when(i == 0)
    def _():
        carry_ref[...] = jnp.zeros(carry_ref.shape, F32)

    mod = _mod_rows(mod_ref)
    row8 = lax.broadcasted_iota(jnp.int32, (8, 1), 0)
    pool_w = [poolw_ref[g].astype(BF16) for g in range(n_pool)]
    causal = (lax.broadcasted_iota(jnp.int32, (SGU_CHUNK, SGU_CHUNK), 0)
              >= lax.broadcasted_iota(jnp.int32, (SGU_CHUNK, SGU_CHUNK), 1))
    sgu_w = [jnp.where(causal, sguw_ref[hh], 0.0).astype(BF16) for hh in range(SGU_HEADS)]

    def shifted(cur, level, dist):
        lanes = cur.shape[1]
        prev8 = carry_ref[level, :, pool_width - lanes:]
        carry_ref[level, :, pool_width - lanes:] = cur[sub - 8:, :]
        rolled = pltpu.roll(cur, dist, axis=0)
        if dist == 8:
            return jnp.concatenate([prev8, rolled[8:]], axis=0)
        head = jnp.where(row8 < dist, pltpu.roll(prev8, dist, axis=0), rolled[:8])
        return jnp.concatenate([head, rolled[8:]], axis=0)

    def project(st, r0):
        st["x"] = x_ref[r0:r0 + sub, :]
        h = _pre(st["x"], mod, gpre_ref[k:k + 1, :]).astype(BF16)
        st["z"] = _dot(h, win_ref[...])

    def pool(st, r0):
        a = st["z"][:, :pool_width]
        pos = (i * tm + r0 + 1 + lax.broadcasted_iota(jnp.int32, (sub, 1), 0)).astype(F32)
        level_sum = a
        y_a = []
        for g, w in enumerate(POOL_WINDOWS):
            level_sum = level_sum + shifted(level_sum, g, w // 2)
            mean = level_sum[:, :gd] / jnp.minimum(pos, float(w))
            dg = (mean - a[:, g * gd:(g + 1) * gd]).astype(BF16)
            y_a.append(_dot(dg, pool_w[g]) * pscale_ref[:, g * gd:(g + 1) * gd])
            level_sum = level_sum[:, gd:]
        st["y_a"] = jnp.concatenate(y_a, axis=-1).astype(BF16)

    def gate(st, r0):
        zb = jax.nn.gelu(st["z"][:, pool_width:])
        sgu_width = zb.shape[1] // 2
        hd = sgu_width // SGU_HEADS
        u = zb[:, :sgu_width]
        v = zb[:, sgu_width:]
        y_b = []
        for hh in range(SGU_HEADS):
            cols = slice(hh * hd, (hh + 1) * hd)
            vh = v[:, cols]
            mu = jnp.mean(vh, axis=-1, keepdims=True)
            var = jnp.mean(jnp.square(vh - mu), axis=-1, keepdims=True)
            vn = ((vh - mu) * lax.rsqrt(var + EPS) * lng_ref[:, cols] + lnb_ref[:, cols]).astype(BF16)
            s_h = [_dot(sgu_w[hh], vn[ck * SGU_CHUNK:(ck + 1) * SGU_CHUNK]) + sgub_ref[hh]
                   for ck in range(sub // SGU_CHUNK)]
            y_b.append(u[:, cols] * jnp.concatenate(s_h, axis=0))
        st["y_b"] = jnp.concatenate(y_b, axis=-1).astype(BF16)

    def finish(st, r0):
        y = _dot(st["y_a"], wout_ref[:pool_width, :]) + _dot(st["y_b"], wout_ref[pool_width:, :])
        o_ref[r0:r0 + sub, :] = _post(st["x"], y, mod, gpost_ref[k:k + 1, :], 1.0)

    _staggered((project, pool, gate, finish), n_split, sub)


def _mix0(x, mod, norm_pre, norm_post, sub, w_in, pool_w, pool_scale, ln_g, ln_b, sgu_w, sgu_b, w_out):
    bsz, s, d = x.shape
    pool_width = pool_scale.shape[0]
    sgu_width = ln_g.shape[0]
    hd = sgu_width // SGU_HEADS
    tm = min(ROW_TILE, s)
    sgu_b_rows = jnp.broadcast_to(sgu_b[:, :, None], (SGU_HEADS, SGU_CHUNK, hd))
    return pl.pallas_call(
        functools.partial(_mix0_kernel, k=sub[1], pool_width=pool_width, n_split=MIX_SPLIT),
        grid=(bsz, s // tm),
        in_specs=[_x_spec(tm, d), _mod_spec(sub, mod), _norm_spec(sub, d), _norm_spec(sub, d),
                  _const_spec(w_in.shape), _const_spec(pool_w.shape), _const_spec((1, pool_width)),
                  _const_spec((1, sgu_width)), _const_spec((1, sgu_width)),
                  _const_spec(sgu_w.shape), _const_spec(sgu_b_rows.shape), _const_spec(w_out.shape)],
        out_specs=_x_spec(tm, d),
        out_shape=jax.ShapeDtypeStruct(x.shape, F32),
        scratch_shapes=[pltpu.VMEM((len(POOL_WINDOWS), 8, pool_width), F32)],
        compiler_params=_params("arbitrary", "arbitrary"),
    )(x, mod, norm_pre, norm_post, w_in, pool_w,
      pool_scale.reshape(1, pool_width), ln_g.reshape(1, sgu_width), ln_b.reshape(1, sgu_width),
      sgu_w, sgu_b_rows, w_out)


def _s5_in_kernel(x_ref, mod_ref, gpre_ref, w_ref, o_ref, u_ref, *, k, n_split):
    n_slab = o_ref.shape[0]
    sub = x_ref.shape[0] // n_split
    rows = sub // SSM_CHUNK
    mod = _mod_rows(mod_ref)

    def project(st, r0):
        h = _pre(x_ref[r0:r0 + sub, :], mod, gpre_ref[k:k + 1, :]).astype(BF16)
        st["u"] = _dot(h, w_ref[...])

    def relayout(st, r0):
        for j in range(n_slab):
            u_ref[j, r0:r0 + sub, :] = st["u"][:, j * LANES:(j + 1) * LANES]
        c0 = r0 // SSM_CHUNK
        for l in range(SSM_CHUNK):
            for j in range(n_slab):
                o_ref[j, c0:c0 + rows, l * LANES:(l + 1) * LANES] = (
                    u_ref[j, pl.ds(r0 + l, rows, stride=SSM_CHUNK), :].astype(BF16))

    _staggered((project, relayout), n_split, sub)


def _s5_in(x, mod, norm_pre, sub, w_in):
    bsz, s, d = x.shape
    width = w_in.shape[1]
    n_slab = width // LANES
    tm = min(S5_ROW_TILE, s)
    rows = tm // SSM_CHUNK
    tiles = s // tm
    return pl.pallas_call(
        functools.partial(_s5_in_kernel, k=sub[1], n_split=S5_SPLIT),
        grid=(bsz, tiles),
        in_specs=[_x_spec(tm, d), _mod_spec(sub, mod), _norm_spec(sub, d), _const_spec(w_in.shape)],
        out_specs=pl.BlockSpec((n_slab, rows, SSM_CHUNK * LANES), lambda b, i: (0, b * tiles + i, 0)),
        out_shape=jax.ShapeDtypeStruct((n_slab, bsz * s // SSM_CHUNK, SSM_CHUNK * LANES), BF16),
        scratch_shapes=[pltpu.VMEM((n_slab, tm, LANES), F32)],
        compiler_params=_params("arbitrary", "arbitrary"),
    )(x, mod, norm_pre, w_in)


def _s5_core_kernel(u_ref, lam_ref, bt_ref, ct_ref, d_ref, o_ref, st_ref, p_ref, plo_ref, qt_ref, toep_ref,
                    *, bsz, n_chunk):
    n_lt = st_ref.shape[0]
    half = n_lt // 2
    n_st = half * LANES
    u = u_ref[...]

    lam_re, lam_im = lam_ref[0:1], lam_ref[1:2]
    dt = jnp.exp(lam_ref[2:3])
    tau = lax.broadcasted_iota(jnp.int32, (SSM_CHUNK + 8, 1), 0).astype(F32)
    mag = jnp.exp(lam_re * dt * tau)
    ang = lam_im * dt * tau
    ap_re, ap_im = mag * jnp.cos(ang), mag * jnp.sin(ang)
    num_re, num_im = ap_re[1:2] - 1.0, ap_im[1:2]
    den = lam_re * lam_re + lam_im * lam_im
    cf_re = (num_re * lam_re + num_im * lam_im) / den
    cf_im = (num_im * lam_re - num_re * lam_im) / den

    n_state = n_st // SLAB_GROUPS
    same_group = (lax.broadcasted_iota(jnp.int32, (LANES, n_st), 0) // SSM_GROUP
                  == lax.broadcasted_iota(jnp.int32, (LANES, n_st), 1) // n_state)

    def block_diag(w):
        return jnp.where(same_group, jnp.concatenate([w] * (n_st // w.shape[1]), axis=-1), 0.0)

    bt_re, bt_im = block_diag(bt_ref[0]), block_diag(bt_ref[1])
    bb_re, bb_im = cf_re * bt_re - cf_im * bt_im, cf_re * bt_im + cf_im * bt_re
    ct_re, ct_im = block_diag(ct_ref[0]), block_diag(ct_ref[1])

    nt = (((1,), (1,)), ((), ()))
    for l in range(SSM_CHUNK):
        blk = slice(l * LANES, (l + 1) * LANES)
        a_r, a_i = ap_re[SSM_CHUNK - 1 - l:SSM_CHUNK - l], ap_im[SSM_CHUNK - 1 - l:SSM_CHUNK - l]
        for part, cols in ((a_r * bb_re - a_i * bb_im, slice(0, n_st)),
                           (a_r * bb_im + a_i * bb_re, slice(n_st, 2 * n_st))):
            hi = part.astype(BF16)
            p_ref[blk, cols] = hi
            plo_ref[blk, cols] = (part - hi.astype(F32)).astype(BF16)

    c_nt = jnp.concatenate([ct_re, -ct_im], axis=-1)
    c_hi = c_nt.astype(BF16)
    c_lo = (c_nt - c_hi.astype(F32)).astype(BF16)
    imp_hi = lax.dot_general(p_ref[...], jnp.concatenate([c_hi, c_lo], axis=0), nt, preferred_element_type=F32)
    imp = (imp_hi[:, :LANES] + imp_hi[:, LANES:]
           + lax.dot_general(plo_ref[...], c_hi, nt, preferred_element_type=F32))
    lag0 = (SSM_CHUNK - 1) * LANES
    toep_ref[:, LANES:] = imp.astype(BF16)
    toep_ref[:lag0, :LANES] = imp[LANES:].astype(BF16)
    toep_ref[lag0:, :LANES] = jnp.zeros((LANES, LANES), BF16)

    seq_rows = st_ref.shape[1] // bsz
    z = _dot(u, p_ref[...])
    for k in range(n_lt):
        for b in range(bsz):
            st_ref[k, b * seq_rows:b * seq_rows + n_chunk, :] = (
                z[b * n_chunk:(b + 1) * n_chunk, k * LANES:(k + 1) * LANES])

    n_tile = u.shape[1] // MXU_DIM
    col_tiles = [slice(n * MXU_DIM, (n + 1) * MXU_DIM) for n in range(n_tile)]
    d_row = jnp.concatenate([d_ref[...]] * (MXU_DIM // LANES), axis=-1)
    y_local = [_dot(u[:, :(n + 1) * MXU_DIM], toep_ref[(n_tile - 1 - n) * MXU_DIM:, :])
               + d_row * u[:, cols].astype(F32) for n, cols in enumerate(col_tiles)]

    for l in range(SSM_CHUNK):
        blk = slice(l * LANES, (l + 1) * LANES)
        a_r, a_i = ap_re[l + 1:l + 2], ap_im[l + 1:l + 2]
        qt_ref[blk, :n_st] = (a_r * ct_re - a_i * ct_im).astype(BF16)
        qt_ref[blk, n_st:] = (-(a_r * ct_im + a_i * ct_re)).astype(BF16)

    a_re = [ap_re[SSM_CHUNK:SSM_CHUNK + 1, k * LANES:(k + 1) * LANES] for k in range(half)]
    a_im = [ap_im[SSM_CHUNK:SSM_CHUNK + 1, k * LANES:(k + 1) * LANES] for k in range(half)]
    s_re = [jnp.zeros((bsz, LANES), F32)] * half
    s_im = [jnp.zeros((bsz, LANES), F32)] * half
    for c in range(n_chunk):
        rows = pl.ds(c, bsz, stride=seq_rows)
        for k in range(half):
            z_re = st_ref[k, rows, :]
            z_im = st_ref[half + k, rows, :]
            st_ref[k, rows, :] = s_re[k]
            st_ref[half + k, rows, :] = s_im[k]
            s_re[k], s_im[k] = (a_re[k] * s_re[k] - a_im[k] * s_im[k] + z_re,
                                a_re[k] * s_im[k] + a_im[k] * s_re[k] + z_im)
    st = jnp.concatenate(
        [jnp.concatenate([st_ref[k, b * seq_rows:b * seq_rows + n_chunk, :] for b in range(bsz)], axis=0)
         .astype(BF16) for k in range(n_lt)], axis=-1)

    for n, cols in enumerate(col_tiles):
        y = y_local[n] + lax.dot_general(st, qt_ref[cols, :], nt, preferred_element_type=F32)
        o_ref[:, cols] = jax.nn.gelu(y).astype(BF16)


def _s5_core(u_cm, lam_rows, bt, ct, d_cm, bsz):
    n_slab, rows, kdim = u_cm.shape
    n_st = lam_rows.shape[2]
    slab = lambda shape: pl.BlockSpec((None,) + shape, lambda j: (j,) + (0,) * len(shape))
    return pl.pallas_call(
        functools.partial(_s5_core_kernel, bsz=bsz, n_chunk=rows // bsz),
        grid=(n_slab,),
        in_specs=[slab((rows, kdim)), slab(lam_rows.shape[1:]), slab(bt.shape[1:]), slab(ct.shape[1:]),
                  slab(d_cm.shape[1:])],
        out_specs=slab((rows, kdim)),
        out_shape=jax.ShapeDtypeStruct(u_cm.shape, BF16),
        scratch_shapes=[pltpu.VMEM((2 * n_st // LANES, rows + SUBLANES * bsz, LANES), F32),
                        pltpu.VMEM((kdim, 2 * n_st), BF16),
                        pltpu.VMEM((kdim, 2 * n_st), BF16),
                        pltpu.VMEM((kdim, 2 * n_st), BF16),
                        pltpu.VMEM((kdim, 2 * LANES), BF16)],
        compiler_params=_params("arbitrary"),
    )(u_cm, lam_rows, bt, ct, d_cm)


def _s5_out_kernel(x_ref, mod_ref, gpost_ref, g_ref, w_ref, o_ref, g_nat_ref, *, k, n_split):
    n_slab = g_ref.shape[0]
    sub = x_ref.shape[0] // n_split
    rows = sub // SSM_CHUNK
    d = w_ref.shape[1] // 2
    mod = _mod_rows(mod_ref)

    def relayout(st, r0):
        c0 = r0 // SSM_CHUNK
        for l in range(SSM_CHUNK):
            for j in range(n_slab):
                g_nat_ref[j, pl.ds(r0 + l, rows, stride=SSM_CHUNK), :] = (
                    g_ref[j, c0:c0 + rows, l * LANES:(l + 1) * LANES].astype(F32))

    def glu(st, r0):
        g = jnp.concatenate([g_nat_ref[j, r0:r0 + sub, :].astype(BF16) for j in range(n_slab)], axis=-1)
        st["ab"] = _dot(g, w_ref[...])

    def finish(st, r0):
        y = st["ab"][:, :d] * _sigmoid(st["ab"][:, d:])
        o_ref[r0:r0 + sub, :] = _post(x_ref[r0:r0 + sub, :], y, mod, gpost_ref[k:k + 1, :], 1.0)

    _staggered((relayout, glu, finish), n_split, sub)


def _s5_out(x, mod, norm_post, sub, g_cm, w_glu):
    bsz, s, d = x.shape
    n_slab = g_cm.shape[0]
    tm = min(S5_ROW_TILE, s)
    rows = tm // SSM_CHUNK
    tiles = s // tm
    return pl.pallas_call(
        functools.partial(_s5_out_kernel, k=sub[1], n_split=S5_SPLIT),
        grid=(bsz, tiles),
        in_specs=[_x_spec(tm, d), _mod_spec(sub, mod), _norm_spec(sub, d),
                  pl.BlockSpec((n_slab, rows, SSM_CHUNK * LANES), lambda b, i: (0, b * tiles + i, 0)),
                  _const_spec(w_glu.shape)],
        out_specs=_x_spec(tm, d),
        out_shape=jax.ShapeDtypeStruct(x.shape, F32),
        scratch_shapes=[pltpu.VMEM((n_slab, tm, LANES), F32)],
        compiler_params=_params("arbitrary", "arbitrary"),
    )(x, mod, norm_post, g_cm, w_glu)


def _s5_slab_params(lam_re, lam_im, b_re, b_im, c_re, c_im, d_skip, log_dt):
    n_group, n_state = lam_re.shape
    n_slab = n_group // SLAB_GROUPS
    n_st = SLAB_GROUPS * n_state
    log_dt_st = jnp.broadcast_to(log_dt[:, None], (n_group, n_state))
    lam_rows = jnp.stack([v.reshape(n_slab, n_st) for v in (lam_re, lam_im, log_dt_st)], axis=1)

    def rows_by_channel(w_re, w_im):
        w = jnp.stack([w_re, w_im], axis=0).reshape(2, n_slab, LANES, n_state).transpose(1, 0, 2, 3)
        return jnp.concatenate([w] * (LANES // n_state), axis=-1)

    bt = rows_by_channel(b_re.transpose(0, 2, 1), b_im.transpose(0, 2, 1))
    ct = rows_by_channel(c_re, c_im)
    return lam_rows, bt, ct, d_skip.reshape(n_slab, 1, LANES)


def kernel(x, c, ada_w, ada_b, norm_pre, norm_post, ffn_w_in, ffn_w_out, ab_w_in, pool_w, pool_scale, sgu_ln_g, sgu_ln_b, sgu_w, sgu_b, ab_w_out, ssm_w_in, ssm_lam_re, ssm_lam_im, ssm_b_re, ssm_b_im, ssm_c_re, ssm_c_im, ssm_d, ssm_log_dt, ssm_w_glu):
    bsz, s, d = x.shape
    depth = ada_w.shape[0]
    assert all(s % min(tile, s) == 0 for tile in (ROW_TILE, S5_ROW_TILE, FFN_ROW_TILE))
    mod = _ada_mod(c, ada_w, ada_b)

    def ffn_casts(l, half):
        return [(ffn_w_in, (l, half)), (ffn_w_out, (l, half))]

    def mixer_casts(l):
        i = l // 2
        return [(ab_w_in, (i,)), (ab_w_out, (i,))] if l % 2 == 0 else [(ssm_w_in, (i,)), (ssm_w_glu, (i,))]

    w_ffn = [ffn_w_in[0, 0].astype(BF16), ffn_w_out[0, 0].astype(BF16)]
    for l in range(depth):
        i = l // 2
        x, cast = _ffn(x, mod, norm_pre, norm_post, (l, 0), *w_ffn, casts=mixer_casts(l) + ffn_casts(l, 1))
        w_mix, w_ffn = cast[:2], cast[2:]
        if l % 2 == 0:
            x = _mix0(x, mod, norm_pre, norm_post, (l, 1), w_mix[0], pool_w[i], pool_scale[i],
                      sgu_ln_g[i], sgu_ln_b[i], sgu_w[i], sgu_b[i], w_mix[1])
        else:
            slab_params = _s5_slab_params(ssm_lam_re[i], ssm_lam_im[i], ssm_b_re[i], ssm_b_im[i],
                                          ssm_c_re[i], ssm_c_im[i], ssm_d[i], ssm_log_dt[i])
            u_cm = _s5_in(x, mod, norm_pre, (l, 1), w_mix[0])
            g_cm = _s5_core(u_cm, *slab_params, bsz)
            x = _s5_out(x, mod, norm_post, (l, 1), g_cm, w_mix[1])
        x, w_ffn = _ffn(x, mod, norm_pre, norm_post, (l, 2), *w_ffn,
                        casts=ffn_casts(l + 1, 0) if l + 1 < depth else [])
    return x
```

```python
import functools

import jax
import jax.numpy as jnp
from jax import lax
from jax.experimental import pallas as pl
from jax.experimental.pallas import tpu as pltpu

F32 = jnp.float32
BF16 = jnp.bfloat16
EPS = 1e-6
N_SUB = 3
FFN_RES_WEIGHT = 0.5
POOL_WINDOWS = (2, 4, 8, 16)
SGU_HEADS = 4
SGU_CHUNK = 128
SSM_GROUP = 16
SSM_CHUNK = 8
LANES = 128
SUBLANES = 8
SLAB_GROUPS = LANES // SSM_GROUP
MXU_DIM = 256
VMEM_MIB = dict(ada=26, ffn=42, mix0=28, s5_in=24, s5_core=42, s5_out=36)
ROW_TILE, MIX_SPLIT = 1024, 2
S5_ROW_TILE, S5_SPLIT = 1024, 1
FFN_ROW_TILE, FFN_SPLIT = 1024, 2


def _const_spec(shape, lead=()):
    nd = len(shape)
    return pl.BlockSpec((None,) * len(lead) + tuple(shape), lambda *_: tuple(lead) + (0,) * nd,
                        pipeline_mode=pl.Buffered(1))


def _params(call, *sem):
    return pltpu.CompilerParams(dimension_semantics=sem, vmem_limit_bytes=VMEM_MIB[call] * 1024 * 1024)


def _rms(x, g):
    return x * lax.rsqrt(jnp.mean(x * x, axis=-1, keepdims=True) + EPS) * g


def _mod_rows(mod_ref):
    b = pl.program_id(0)
    return [mod_ref[r, pl.ds(b, 1), :] for r in range(3)]


def _pre(x, mod, g_pre):
    return _rms(x, g_pre * (1.0 + mod[1])) + mod[0]


def _post(x, y, mod, g_post, res_weight):
    return x + _rms(y, res_weight * mod[2] * g_post)


def _sigmoid(v):
    return 1.0 / (1.0 + jnp.exp(-v))


def _dot(a, b):
    return jnp.dot(a, b, preferred_element_type=F32)


_NT = (((1,), (1,)), ((), ()))


def _staggered(stages, n_split, sub):
    state = [{} for _ in range(n_split)]
    for t in range(len(stages) + n_split - 1):
        for p in range(n_split):
            if 0 <= t - p < len(stages):
                stages[t - p](state[p], p * sub)


def _ada_kernel(c_ref, w_ref, b_ref, o_ref):
    c = c_ref[...]
    d = c.shape[1]
    cond = (c * _sigmoid(c)).astype(BF16)
    for r in range(o_ref.shape[0]):
        cols = slice(r * d, (r + 1) * d)
        o_ref[r] = _dot(cond, w_ref[:, cols].astype(BF16)) + b_ref[:, cols]


def _ada_mod(c, ada_w, ada_b):
    depth, d, n = ada_w.shape
    bsz = c.shape[0]
    out = pl.pallas_call(
        _ada_kernel,
        grid=(depth, N_SUB),
        in_specs=[
            pl.BlockSpec((bsz, d), lambda l, j: (0, 0)),
            pl.BlockSpec((None, d, 3 * d), lambda l, j: (l, 0, j)),
            pl.BlockSpec((None, 1, 3 * d), lambda l, j: (l, 0, j)),
        ],
        out_specs=pl.BlockSpec((None, 3, bsz, d), lambda l, j: (l, j, 0, 0)),
        out_shape=jax.ShapeDtypeStruct((depth, n // d, bsz, d), F32),
        compiler_params=_params("ada", "arbitrary", "arbitrary"),
    )(c, ada_w, ada_b.reshape(depth, 1, n))
    return out.reshape(depth, N_SUB, 3, bsz, d)


def _x_spec(tm, d):
    return pl.BlockSpec((None, tm, d), lambda b, i: (b, i, 0))


def _mod_spec(sub, mod):
    return _const_spec(mod.shape[2:], lead=sub)


def _norm_spec(sub, d):
    return _const_spec((N_SUB, d), lead=sub[:1])


def _ffn_kernel(x_ref, mod_ref, gpre_ref, gpost_ref, win_ref, wout_ref, *rest, k, d_ff, fc, n_split):
    n_cast = len(rest) // 2
    o_ref = rest[n_cast]
    for src_ref, dst_ref in zip(rest[:n_cast], rest[n_cast + 1:]):
        dst_ref[...] = src_ref[...].astype(BF16)

    mod = _mod_rows(mod_ref)
    sub = x_ref.shape[0] // n_split
    rows = [slice(p * sub, (p + 1) * sub) for p in range(n_split)]
    h = [_pre(x_ref[r, :], mod, gpre_ref[k:k + 1, :]).astype(BF16) for r in rows]
    acc = [jnp.zeros((sub, x_ref.shape[1]), F32)] * n_split
    for j in range(d_ff // fc):
        for p in range(n_split):
            a = _dot(h[p], win_ref[:, j * fc:(j + 1) * fc])
            b = _dot(h[p], win_ref[:, d_ff + j * fc:d_ff + (j + 1) * fc])
            act = (a * _sigmoid(a) * b).astype(BF16)
            acc[p] = acc[p] + _dot(act, wout_ref[j * fc:(j + 1) * fc, :])
    for p, r in enumerate(rows):
        o_ref[r, :] = _post(x_ref[r, :], acc[p], mod, gpost_ref[k:k + 1, :], FFN_RES_WEIGHT)


def _ffn(x, mod, norm_pre, norm_post, sub, w_in, w_out, casts=()):
    bsz, s, d = x.shape
    d_ff = w_out.shape[0]
    tm = min(FFN_ROW_TILE, s)
    tiles = s // tm
    steps = bsz * tiles
    cast_in, cast_out, cast_shapes = [], [], []
    for w, lead in casts:
        r, c = w.shape[len(lead):]
        cast_in.append(pl.BlockSpec((None,) * len(lead) + (r // steps, c),
                                    lambda b, i, lead=lead: lead + (b * tiles + i, 0)))
        cast_out.append(pl.BlockSpec((r // steps, c), lambda b, i: (b * tiles + i, 0)))
        cast_shapes.append(jax.ShapeDtypeStruct((r, c), BF16))
    out = pl.pallas_call(
        functools.partial(_ffn_kernel, k=sub[1], d_ff=d_ff, fc=MXU_DIM, n_split=FFN_SPLIT),
        grid=(bsz, tiles),
        in_specs=[_x_spec(tm, d), _mod_spec(sub, mod), _norm_spec(sub, d), _norm_spec(sub, d),
                  pl.BlockSpec(memory_space=pltpu.VMEM), pl.BlockSpec(memory_space=pltpu.VMEM)] + cast_in,
        out_specs=[pl.BlockSpec((None, tm, d), lambda b, i: (b, i, 0), pipeline_mode=pl.Buffered(1))] + cast_out,
        out_shape=[jax.ShapeDtypeStruct(x.shape, F32)] + cast_shapes,
        compiler_params=_params("ffn", "arbitrary", "arbitrary"),
    )(x, mod, norm_pre, norm_post, w_in, w_out, *[w for w, _ in casts])
    return out[0], list(out[1:])


def _mix0_kernel(x_ref, mod_ref, gpre_ref, gpost_ref, win_ref, poolw_ref, pscale_ref, lng_ref, lnb_ref,
                 sguw_ref, sgub_ref, wout_ref, o_ref, carry_ref, *, k, pool_width, n_split):
    i = pl.program_id(1)
    tm = x_ref.shape[0]
    sub = tm // n_split
    n_pool = len(POOL_WINDOWS)
    gd = pool_width // n_pool

    @pl.when(i == 0)
    def _():
        carry_ref[...] = jnp.zeros(carry_ref.shape, F32)

    mod = _mod_rows(mod_ref)
    row8 = lax.broadcasted_iota(jnp.int32, (8, 1), 0)
    pool_w = [poolw_ref[g].astype(BF16) for g in range(n_pool)]
    causal = (lax.broadcasted_iota(jnp.int32, (SGU_CHUNK, SGU_CHUNK), 0)
              >= lax.broadcasted_iota(jnp.int32, (SGU_CHUNK, SGU_CHUNK), 1))
    sgu_w = [jnp.where(causal, sguw_ref[hh], 0.0).astype(BF16) for hh in range(SGU_HEADS)]

    def shifted(cur, level, dist):
        lanes = cur.shape[1]
        prev8 = carry_ref[level, :, pool_width - lanes:]
        carry_ref[level, :, pool_width - lanes:] = cur[sub - 8:, :]
        rolled = pltpu.roll(cur, dist, axis=0)
        if dist == 8:
            return jnp.concatenate([prev8, rolled[8:]], axis=0)
        head = jnp.where(row8 < dist, pltpu.roll(prev8, dist, axis=0), rolled[:8])
        return jnp.concatenate([head, rolled[8:]], axis=0)

    def project(st, r0):
        st["x"] = x_ref[r0:r0 + sub, :]
        h = _pre(st["x"], mod, gpre_ref[k:k + 1, :]).astype(BF16)
        st["z"] = _dot(h, win_ref[...])

    def pool(st, r0):
        a = st["z"][:, :pool_width]
        pos = (i * tm + r0 + 1 + lax.broadcasted_iota(jnp.int32, (sub, 1), 0)).astype(F32)
        level_sum = a
        y_a = []
        for g, w in enumerate(POOL_WINDOWS):
            level_sum = level_sum + shifted(level_sum, g, w // 2)
            mean = level_sum[:, :gd] / jnp.minimum(pos, float(w))
            dg = (mean - a[:, g * gd:(g + 1) * gd]).astype(BF16)
            y_a.append(_dot(dg, pool_w[g]) * pscale_ref[:, g * gd:(g + 1) * gd])
            level_sum = level_sum[:, gd:]
        st["y_a"] = jnp.concatenate(y_a, axis=-1).astype(BF16)

    def gate(st, r0):
        zb = jax.nn.gelu(st["z"][:, pool_width:])
        sgu_width = zb.shape[1] // 2
        hd = sgu_width // SGU_HEADS
        u = zb[:, :sgu_width]
        v = zb[:, sgu_width:]
        y_b = []
        for hh in range(SGU_HEADS):
            cols = slice(hh * hd, (hh + 1) * hd)
            vh = v[:, cols]
            mu = jnp.mean(vh, axis=-1, keepdims=True)
            var = jnp.mean(jnp.square(vh - mu), axis=-1, keepdims=True)
            vn = ((vh - mu) * lax.rsqrt(var + EPS) * lng_ref[:, cols] + lnb_ref[:, cols]).astype(BF16)
            s_h = [_dot(sgu_w[hh], vn[ck * SGU_CHUNK:(ck + 1) * SGU_CHUNK]) + sgub_ref[hh]
                   for ck in range(sub // SGU_CHUNK)]
            y_b.append(u[:, cols] * jnp.concatenate(s_h, axis=0))
        st["y_b"] = jnp.concatenate(y_b, axis=-1).astype(BF16)

    def finish(st, r0):
        y = _dot(st["y_a"], wout_ref[:pool_width, :]) + _dot(st["y_b"], wout_ref[pool_width:, :])
        o_ref[r0:r0 + sub, :] = _post(st["x"], y, mod, gpost_ref[k:k + 1, :], 1.0)

    _staggered((project, pool, gate, finish), n_split, sub)


def _mix0(x, mod, norm_pre, norm_post, sub, w_in, pool_w, pool_scale, ln_g, ln_b, sgu_w, sgu_b, w_out):
    bsz, s, d = x.shape
    pool_width = pool_scale.shape[0]
    sgu_width = ln_g.shape[0]
    hd = sgu_width // SGU_HEADS
    tm = min(ROW_TILE, s)
    sgu_b_rows = jnp.broadcast_to(sgu_b[:, :, None], (SGU_HEADS, SGU_CHUNK, hd))
    return pl.pallas_call(
        functools.partial(_mix0_kernel, k=sub[1], pool_width=pool_width, n_split=MIX_SPLIT),
        grid=(bsz, s // tm),
        in_specs=[_x_spec(tm, d), _mod_spec(sub, mod), _norm_spec(sub, d), _norm_spec(sub, d),
                  _const_spec(w_in.shape), _const_spec(pool_w.shape), _const_spec((1, pool_width)),
                  _const_spec((1, sgu_width)), _const_spec((1, sgu_width)),
                  _const_spec(sgu_w.shape), _const_spec(sgu_b_rows.shape), _const_spec(w_out.shape)],
        out_specs=_x_spec(tm, d),
        out_shape=jax.ShapeDtypeStruct(x.shape, F32),
        scratch_shapes=[pltpu.VMEM((len(POOL_WINDOWS), 8, pool_width), F32)],
        compiler_params=_params("mix0", "arbitrary", "arbitrary"),
    )(x, mod, norm_pre, norm_post, w_in, pool_w,
      pool_scale.reshape(1, pool_width), ln_g.reshape(1, sgu_width), ln_b.reshape(1, sgu_width),
      sgu_w, sgu_b_rows, w_out)


def _s5_in_kernel(x_ref, mod_ref, gpre_ref, w_ref, o_ref, u_ref, *, k, n_split):
    n_slab = o_ref.shape[0]
    sub = x_ref.shape[0] // n_split
    rows = sub // SSM_CHUNK
    mod = _mod_rows(mod_ref)

    def project(st, r0):
        h = _pre(x_ref[r0:r0 + sub, :], mod, gpre_ref[k:k + 1, :]).astype(BF16)
        st["u"] = _dot(h, w_ref[...])

    def relayout(st, r0):
        for j in range(n_slab):
            u_ref[j, r0:r0 + sub, :] = st["u"][:, j * LANES:(j + 1) * LANES]
        c0 = r0 // SSM_CHUNK
        for l in range(SSM_CHUNK):
            for j in range(n_slab):
                o_ref[j, c0:c0 + rows, l * LANES:(l + 1) * LANES] = (
                    u_ref[j, pl.ds(r0 + l, rows, stride=SSM_CHUNK), :].astype(BF16))

    _staggered((project, relayout), n_split, sub)


def _s5_in(x, mod, norm_pre, sub, w_in):
    bsz, s, d = x.shape
    width = w_in.shape[1]
    n_slab = width // LANES
    tm = min(S5_ROW_TILE, s)
    rows = tm // SSM_CHUNK
    tiles = s // tm
    return pl.pallas_call(
        functools.partial(_s5_in_kernel, k=sub[1], n_split=S5_SPLIT),
        grid=(bsz, tiles),
        in_specs=[_x_spec(tm, d), _mod_spec(sub, mod), _norm_spec(sub, d), _const_spec(w_in.shape)],
        out_specs=pl.BlockSpec((n_slab, rows, SSM_CHUNK * LANES), lambda b, i: (0, b * tiles + i, 0)),
        out_shape=jax.ShapeDtypeStruct((n_slab, bsz * s // SSM_CHUNK, SSM_CHUNK * LANES), BF16),
        scratch_shapes=[pltpu.VMEM((n_slab, tm, LANES), F32)],
        compiler_params=_params("s5_in", "arbitrary", "arbitrary"),
    )(x, mod, norm_pre, w_in)


def _s5_core_kernel(u_ref, lam_ref, bt_ref, ct_ref, d_ref, o_ref, st_ref, p_ref, plo_ref, qt_ref, toep_ref,
                    *, bsz, n_chunk):
    n_lt = st_ref.shape[0]
    half = n_lt // 2
    n_st = half * LANES
    u = u_ref[...]

    lam_re, lam_im = lam_ref[0:1], lam_ref[1:2]
    dt = jnp.exp(lam_ref[2:3])
    tau = lax.broadcasted_iota(jnp.int32, (SSM_CHUNK + 8, 1), 0).astype(F32)
    mag = jnp.exp(lam_re * dt * tau)
    ang = lam_im * dt * tau
    ap_re, ap_im = mag * jnp.cos(ang), mag * jnp.sin(ang)
    num_re, num_im = ap_re[1:2] - 1.0, ap_im[1:2]
    den = lam_re * lam_re + lam_im * lam_im
    cf_re = (num_re * lam_re + num_im * lam_im) / den
    cf_im = (num_im * lam_re - num_re * lam_im) / den

    n_state = n_st // SLAB_GROUPS
    same_group = (lax.broadcasted_iota(jnp.int32, (LANES, n_st), 0) // SSM_GROUP
                  == lax.broadcasted_iota(jnp.int32, (LANES, n_st), 1) // n_state)

    def block_diag(w):
        return jnp.where(same_group, jnp.concatenate([w] * (n_st // w.shape[1]), axis=-1), 0.0)

    bt_re, bt_im = block_diag(bt_ref[0]), block_diag(bt_ref[1])
    bb_re, bb_im = cf_re * bt_re - cf_im * bt_im, cf_re * bt_im + cf_im * bt_re
    ct_re, ct_im = block_diag(ct_ref[0]), block_diag(ct_ref[1])

    for l in range(SSM_CHUNK):
        blk = slice(l * LANES, (l + 1) * LANES)
        a_r, a_i = ap_re[SSM_CHUNK - 1 - l:SSM_CHUNK - l], ap_im[SSM_CHUNK - 1 - l:SSM_CHUNK - l]
        for part, cols in ((a_r * bb_re - a_i * bb_im, slice(0, n_st)),
                           (a_r * bb_im + a_i * bb_re, slice(n_st, 2 * n_st))):
            hi = part.astype(BF16)
            p_ref[blk, cols] = hi
            plo_ref[blk, cols] = (part - hi.astype(F32)).astype(BF16)

    c_nt = jnp.concatenate([ct_re, -ct_im], axis=-1)
    c_hi = c_nt.astype(BF16)
    c_lo = (c_nt - c_hi.astype(F32)).astype(BF16)
    imp_hi = lax.dot_general(p_ref[...], jnp.concatenate([c_hi, c_lo], axis=0), _NT,
                             preferred_element_type=F32)
    imp = (imp_hi[:, :LANES] + imp_hi[:, LANES:]
           + lax.dot_general(plo_ref[...], c_hi, _NT, preferred_element_type=F32))
    lag0 = (SSM_CHUNK - 1) * LANES
    toep_ref[:, LANES:] = imp.astype(BF16)
    toep_ref[:lag0, :LANES] = imp[LANES:].astype(BF16)
    toep_ref[lag0:, :LANES] = jnp.zeros((LANES, LANES), BF16)

    seq_rows = st_ref.shape[1] // bsz
    z = _dot(u, p_ref[...])
    for k in range(n_lt):
        for b in range(bsz):
            st_ref[k, b * seq_rows:b * seq_rows + n_chunk, :] = (
                z[b * n_chunk:(b + 1) * n_chunk, k * LANES:(k + 1) * LANES])

    n_tile = u.shape[1] // MXU_DIM
    col_tiles = [slice(n * MXU_DIM, (n + 1) * MXU_DIM) for n in range(n_tile)]
    d_row = jnp.concatenate([d_ref[...]] * (MXU_DIM // LANES), axis=-1)
    y_local = [_dot(u[:, :(n + 1) * MXU_DIM], toep_ref[(n_tile - 1 - n) * MXU_DIM:, :])
               + d_row * u[:, cols].astype(F32) for n, cols in enumerate(col_tiles)]

    for l in range(SSM_CHUNK):
        blk = slice(l * LANES, (l + 1) * LANES)
        a_r, a_i = ap_re[l + 1:l + 2], ap_im[l + 1:l + 2]
        qt_ref[blk, :n_st] = (a_r * ct_re - a_i * ct_im).astype(BF16)
        qt_ref[blk, n_st:] = (-(a_r * ct_im + a_i * ct_re)).astype(BF16)

    a_re = [ap_re[SSM_CHUNK:SSM_CHUNK + 1, k * LANES:(k + 1) * LANES] for k in range(half)]
    a_im = [ap_im[SSM_CHUNK:SSM_CHUNK + 1, k * LANES:(k + 1) * LANES] for k in range(half)]
    s_re = [jnp.zeros((bsz, LANES), F32)] * half
    s_im = [jnp.zeros((bsz, LANES), F32)] * half
    for c in range(n_chunk):
        rows = pl.ds(c, bsz, stride=seq_rows)
        for k in range(half):
            z_re = st_ref[k, rows, :]
            z_im = st_ref[half + k, rows, :]
            st_ref[k, rows, :] = s_re[k]
            st_ref[half + k, rows, :] = s_im[k]
            s_re[k], s_im[k] = (a_re[k] * s_re[k] - a_im[k] * s_im[k] + z_re,
                                a_re[k] * s_im[k] + a_im[k] * s_re[k] + z_im)
    st = jnp.concatenate(
        [jnp.concatenate([st_ref[k, b * seq_rows:b * seq_rows + n_chunk, :] for b in range(bsz)], axis=0)
         .astype(BF16) for k in range(n_lt)], axis=-1)

    for n, cols in enumerate(col_tiles):
        y = y_local[n] + lax.dot_general(st, qt_ref[cols, :], _NT, preferred_element_type=F32)
        o_ref[:, cols] = jax.nn.gelu(y).astype(BF16)


def _s5_core(u_cm, lam_rows, bt, ct, d_cm, bsz):
    n_slab, rows, kdim = u_cm.shape
    n_st = lam_rows.shape[2]
    slab = lambda shape: pl.BlockSpec((None,) + shape, lambda j: (j,) + (0,) * len(shape))
    return pl.pallas_call(
        functools.partial(_s5_core_kernel, bsz=bsz, n_chunk=rows // bsz),
        grid=(n_slab,),
        in_specs=[slab((rows, kdim)), slab(lam_rows.shape[1:]), slab(bt.shape[1:]), slab(ct.shape[1:]),
                  slab(d_cm.shape[1:])],
        out_specs=slab((rows, kdim)),
        out_shape=jax.ShapeDtypeStruct(u_cm.shape, BF16),
        scratch_shapes=[pltpu.VMEM((2 * n_st // LANES, rows + SUBLANES * bsz, LANES), F32),
                        pltpu.VMEM((kdim, 2 * n_st), BF16),
                        pltpu.VMEM((kdim, 2 * n_st), BF16),
                        pltpu.VMEM((kdim, 2 * n_st), BF16),
                        pltpu.VMEM((kdim, 2 * LANES), BF16)],
        compiler_params=_params("s5_core", "arbitrary"),
    )(u_cm, lam_rows, bt, ct, d_cm)


def _s5_out_kernel(x_ref, mod_ref, gpost_ref, g_ref, w_ref, o_ref, g_nat_ref, *, k, n_split):
    n_slab = g_ref.shape[0]
    sub = x_ref.shape[0] // n_split
    rows = sub // SSM_CHUNK
    d = w_ref.shape[1] // 2
    mod = _mod_rows(mod_ref)

    def relayout(st, r0):
        c0 = r0 // SSM_CHUNK
        for l in range(SSM_CHUNK):
            for j in range(n_slab):
                g_nat_ref[j, pl.ds(r0 + l, rows, stride=SSM_CHUNK), :] = (
                    g_ref[j, c0:c0 + rows, l * LANES:(l + 1) * LANES].astype(F32))

    def glu(st, r0):
        g = jnp.concatenate([g_nat_ref[j, r0:r0 + sub, :].astype(BF16) for j in range(n_slab)], axis=-1)
        st["ab"] = _dot(g, w_ref[...])

    def finish(st, r0):
        y = st["ab"][:, :d] * _sigmoid(st["ab"][:, d:])
        o_ref[r0:r0 + sub, :] = _post(x_ref[r0:r0 + sub, :], y, mod, gpost_ref[k:k + 1, :], 1.0)

    _staggered((relayout, glu, finish), n_split, sub)


def _s5_out(x, mod, norm_post, sub, g_cm, w_glu):
    bsz, s, d = x.shape
    n_slab = g_cm.shape[0]
    tm = min(S5_ROW_TILE, s)
    rows = tm // SSM_CHUNK
    tiles = s // tm
    return pl.pallas_call(
        functools.partial(_s5_out_kernel, k=sub[1], n_split=S5_SPLIT),
        grid=(bsz, tiles),
        in_specs=[_x_spec(tm, d), _mod_spec(sub, mod), _norm_spec(sub, d),
                  pl.BlockSpec((n_slab, rows, SSM_CHUNK * LANES), lambda b, i: (0, b * tiles + i, 0)),
                  _const_spec(w_glu.shape)],
        out_specs=_x_spec(tm, d),
        out_shape=jax.ShapeDtypeStruct(x.shape, F32),
        scratch_shapes=[pltpu.VMEM((n_slab, tm, LANES), F32)],
        compiler_params=_params("s5_out", "arbitrary", "arbitrary"),
    )(x, mod, norm_post, g_cm, w_glu)


def _s5_slab_params(lam_re, lam_im, b_re, b_im, c_re, c_im, d_skip, log_dt):
    n_group, n_state = lam_re.shape
    n_slab = n_group // SLAB_GROUPS
    n_st = SLAB_GROUPS * n_state
    log_dt_st = jnp.broadcast_to(log_dt[:, None], (n_group, n_state))
    lam_rows = jnp.stack([v.reshape(n_slab, n_st) for v in (lam_re, lam_im, log_dt_st)], axis=1)

    def rows_by_channel(w_re, w_im):
        w = jnp.stack([w_re, w_im], axis=0).reshape(2, n_slab, LANES, n_state).transpose(1, 0, 2, 3)
        return jnp.concatenate([w] * (LANES // n_state), axis=-1)

    bt = rows_by_channel(b_re.transpose(0, 2, 1), b_im.transpose(0, 2, 1))
    ct = rows_by_channel(c_re, c_im)
    return lam_rows, bt, ct, d_skip.reshape(n_slab, 1, LANES)


def kernel(x, c, ada_w, ada_b, norm_pre, norm_post, ffn_w_in, ffn_w_out, ab_w_in, pool_w, pool_scale, sgu_ln_g, sgu_ln_b, sgu_w, sgu_b, ab_w_out, ssm_w_in, ssm_lam_re, ssm_lam_im, ssm_b_re, ssm_b_im, ssm_c_re, ssm_c_im, ssm_d, ssm_log_dt, ssm_w_glu):
    bsz, s, d = x.shape
    depth = ada_w.shape[0]
    assert all(s % min(tile, s) == 0 for tile in (ROW_TILE, S5_ROW_TILE, FFN_ROW_TILE))
    mod = _ada_mod(c, ada_w, ada_b)

    def ffn_casts(l, half):
        return [(ffn_w_in, (l, half)), (ffn_w_out, (l, half))]

    def mixer_casts(l):
        i = l // 2
        return [(ab_w_in, (i,)), (ab_w_out, (i,))] if l % 2 == 0 else [(ssm_w_in, (i,)), (ssm_w_glu, (i,))]

    w_ffn = [ffn_w_in[0, 0].astype(BF16), ffn_w_out[0, 0].astype(BF16)]
    for l in range(depth):
        i = l // 2
        x, cast = _ffn(x, mod, norm_pre, norm_post, (l, 0), *w_ffn, casts=mixer_casts(l) + ffn_casts(l, 1))
        w_mix, w_ffn = cast[:2], cast[2:]
        if l % 2 == 0:
            x = _mix0(x, mod, norm_pre, norm_post, (l, 1), w_mix[0], pool_w[i], pool_scale[i],
                      sgu_ln_g[i], sgu_ln_b[i], sgu_w[i], sgu_b[i], w_mix[1])
        else:
            slab_params = _s5_slab_params(ssm_lam_re[i], ssm_lam_im[i], ssm_b_re[i], ssm_b_im[i],
                                          ssm_c_re[i], ssm_c_im[i], ssm_d[i], ssm_log_dt[i])
            u_cm = _s5_in(x, mod, norm_pre, (l, 1), w_mix[0])
            g_cm = _s5_core(u_cm, *slab_params, bsz)
            x = _s5_out(x, mod, norm_post, (l, 1), g_cm, w_mix[1])
        x, w_ffn = _ffn(x, mod, norm_pre, norm_post, (l, 2), *w_ffn,
                        casts=ffn_casts(l + 1, 0) if l + 1 < depth else [])
    return x
```

```python
import functools

import jax
import jax.numpy as jnp
from jax import lax
from jax.experimental import pallas as pl
from jax.experimental.pallas import tpu as pltpu

F32 = jnp.float32
BF16 = jnp.bfloat16
EPS = 1e-6
N_SUB = 3
FFN_RES_WEIGHT = 0.5
POOL_WINDOWS = (2, 4, 8, 16)
SGU_HEADS = 4
SGU_CHUNK = 128
SSM_GROUP = 16
SSM_CHUNK = 8
LANES = 128
SUBLANES = 8
SLAB_GROUPS = LANES // SSM_GROUP
MXU_DIM = 256
ADA_CAST_BLOCKS = 4
VMEM_MIB = dict(ada=54, ffn=46, mix0=36, s5_in=32, s5_core=48, s5_out=46)
ROW_TILE = 1024
MIX_SPLIT = 2
S5_ROW_TILE = 1024
S5_SPLIT = 1
FFN_ROW_TILE = 1024
FFN_SPLIT = 2


def _const_spec(shape, lead=()):
    nd = len(shape)
    return pl.BlockSpec((None,) * len(lead) + tuple(shape), lambda *_: tuple(lead) + (0,) * nd,
                        pipeline_mode=pl.Buffered(1))


def _params(call, *sem):
    return pltpu.CompilerParams(dimension_semantics=sem, vmem_limit_bytes=VMEM_MIB[call] * 1024 * 1024)


def _rms(x, g):
    return x * lax.rsqrt(jnp.mean(x * x, axis=-1, keepdims=True) + EPS) * g


def _mod_rows(mod_ref):
    b = pl.program_id(0)
    return [mod_ref[r, pl.ds(b, 1), :] for r in range(3)]


def _pre(x, mod, g_pre):
    return _rms(x, g_pre * (1.0 + mod[1])) + mod[0]


def _post(x, y, mod, g_post, res_weight):
    return x + _rms(y, res_weight * mod[2] * g_post)


def _sigmoid(v):
    return 1.0 / (1.0 + jnp.exp(-v))


def _dot(a, b):
    return jnp.dot(a, b, preferred_element_type=F32)


_NT = (((1,), (1,)), ((), ()))


def _staggered(stages, n_split, sub):
    state = [{} for _ in range(n_split)]
    for t in range(len(stages) + n_split - 1):
        for p in range(n_split):
            if 0 <= t - p < len(stages):
                stages[t - p](state[p], p * sub)


def _ada_kernel(c_ref, w_ref, b_ref, *rest):
    n_cast = len(rest) // 2
    o_ref = rest[n_cast]
    for src_ref, dst_ref in zip(rest[:n_cast], rest[n_cast + 1:]):
        dst_ref[...] = src_ref[...].astype(BF16)
    c = c_ref[...]
    d = c.shape[1]
    cond = (c * _sigmoid(c)).astype(BF16)
    for r in range(o_ref.shape[0]):
        cols = slice(r * d, (r + 1) * d)
        o_ref[r] = _dot(cond, w_ref[:, cols].astype(BF16)) + b_ref[:, cols]


def _ada_mod(c, ada_w, ada_b, casts=()):
    depth, d, n = ada_w.shape
    bsz = c.shape[0]
    step = lambda l, j: jnp.minimum(l * N_SUB + j, ADA_CAST_BLOCKS - 1)
    cast_in, cast_out, cast_shapes = [], [], []
    for w, lead in casts:
        r, cc = w.shape[len(lead):]
        cast_in.append(pl.BlockSpec((None,) * len(lead) + (r // ADA_CAST_BLOCKS, cc),
                                    lambda l, j, lead=lead: lead + (step(l, j), 0)))
        cast_out.append(pl.BlockSpec((r // ADA_CAST_BLOCKS, cc), lambda l, j: (step(l, j), 0)))
        cast_shapes.append(jax.ShapeDtypeStruct((r, cc), BF16))
    out = pl.pallas_call(
        _ada_kernel,
        grid=(depth, N_SUB),
        in_specs=[
            pl.BlockSpec((bsz, d), lambda l, j: (0, 0)),
            pl.BlockSpec((None, d, 3 * d), lambda l, j: (l, 0, j)),
            pl.BlockSpec((None, 1, 3 * d), lambda l, j: (l, 0, j)),
        ] + cast_in,
        out_specs=[pl.BlockSpec((None, 3, bsz, d), lambda l, j: (l, j, 0, 0))] + cast_out,
        out_shape=[jax.ShapeDtypeStruct((depth, n // d, bsz, d), F32)] + cast_shapes,
        compiler_params=_params("ada", "arbitrary", "arbitrary"),
    )(c, ada_w, ada_b.reshape(depth, 1, n), *[w for w, _ in casts])
    return out[0].reshape(depth, N_SUB, 3, bsz, d), list(out[1:])


def _x_spec(tm, d):
    return pl.BlockSpec((None, tm, d), lambda b, i: (b, i, 0))


def _mod_spec(sub, mod):
    return _const_spec(mod.shape[2:], lead=sub)


def _norm_spec(sub, d):
    return _const_spec((N_SUB, d), lead=sub[:1])


def _ffn_kernel(x_ref, mod_ref, gpre_ref, gpost_ref, win_ref, wout_ref, *rest, k, d_ff, fc, n_split):
    n_cast = len(rest) // 2
    o_ref = rest[n_cast]
    for src_ref, dst_ref in zip(rest[:n_cast], rest[n_cast + 1:]):
        dst_ref[...] = src_ref[...].astype(BF16)

    mod = _mod_rows(mod_ref)
    sub = x_ref.shape[0] // n_split
    rows = [slice(p * sub, (p + 1) * sub) for p in range(n_split)]
    h = [_pre(x_ref[r, :], mod, gpre_ref[k:k + 1, :]).astype(BF16) for r in rows]
    acc = [jnp.zeros((sub, x_ref.shape[1]), F32)] * n_split
    for j in range(d_ff // fc):
        for p in range(n_split):
            a = _dot(h[p], win_ref[:, j * fc:(j + 1) * fc])
            b = _dot(h[p], win_ref[:, d_ff + j * fc:d_ff + (j + 1) * fc])
            act = (a * _sigmoid(a) * b).astype(BF16)
            acc[p] = acc[p] + _dot(act, wout_ref[j * fc:(j + 1) * fc, :])
    for p, r in enumerate(rows):
        o_ref[r, :] = _post(x_ref[r, :], acc[p], mod, gpost_ref[k:k + 1, :], FFN_RES_WEIGHT)


def _ffn(x, mod, norm_pre, norm_post, sub, w_in, w_out, casts=()):
    bsz, s, d = x.shape
    d_ff = w_out.shape[0]
    tm = min(FFN_ROW_TILE, s)
    tiles = s // tm
    steps = bsz * tiles
    cast_in, cast_out, cast_shapes = [], [], []
    for w, lead in casts:
        r, c = w.shape[len(lead):]
        cast_in.append(pl.BlockSpec((None,) * len(lead) + (r // steps, c),
                                    lambda b, i, lead=lead: lead + (b * tiles + i, 0)))
        cast_out.append(pl.BlockSpec((r // steps, c), lambda b, i: (b * tiles + i, 0)))
        cast_shapes.append(jax.ShapeDtypeStruct((r, c), BF16))
    out = pl.pallas_call(
        functools.partial(_ffn_kernel, k=sub[1], d_ff=d_ff, fc=MXU_DIM, n_split=FFN_SPLIT),
        grid=(bsz, tiles),
        in_specs=[_x_spec(tm, d), _mod_spec(sub, mod), _norm_spec(sub, d), _norm_spec(sub, d),
                  pl.BlockSpec(memory_space=pltpu.VMEM), pl.BlockSpec(memory_space=pltpu.VMEM)] + cast_in,
        out_specs=[_x_spec(tm, d)] + cast_out,
        out_shape=[jax.ShapeDtypeStruct(x.shape, F32)] + cast_shapes,
        compiler_params=_params("ffn", "arbitrary", "arbitrary"),
    )(x, mod, norm_pre, norm_post, w_in, w_out, *[w for w, _ in casts])
    return out[0], list(out[1:])


def _mix0_kernel(x_ref, mod_ref, gpre_ref, gpost_ref, win_ref, poolw_ref, pscale_ref, lng_ref, lnb_ref,
                 sguw_ref, sgub_ref, wout_ref, o_ref, carry_ref, *, k, pool_width, n_split):
    i = pl.program_id(1)
    tm = x_ref.shape[0]
    sub = tm // n_split
    n_pool = len(POOL_WINDOWS)
    gd = pool_width // n_pool

    @pl.when(i == 0)
    def _():
        carry_ref[...] = jnp.zeros(carry_ref.shape, F32)

    mod = _mod_rows(mod_ref)
    row8 = lax.broadcasted_iota(jnp.int32, (8, 1), 0)
    pool_w = [poolw_ref[g].astype(BF16) for g in range(n_pool)]
    causal = (lax.broadcasted_iota(jnp.int32, (SGU_CHUNK, SGU_CHUNK), 0)
              >= lax.broadcasted_iota(jnp.int32, (SGU_CHUNK, SGU_CHUNK), 1))
    sgu_w = [jnp.where(causal, sguw_ref[hh], 0.0).astype(BF16) for hh in range(SGU_HEADS)]

    def shifted(cur, level, dist):
        lanes = cur.shape[1]
        prev8 = carry_ref[level, :, pool_width - lanes:]
        carry_ref[level, :, pool_width - lanes:] = cur[sub - 8:, :]
        rolled = pltpu.roll(cur, dist, axis=0)
        if dist == 8:
            return jnp.concatenate([prev8, rolled[8:]], axis=0)
        head = jnp.where(row8 < dist, pltpu.roll(prev8, dist, axis=0), rolled[:8])
        return jnp.concatenate([head, rolled[8:]], axis=0)

    def project(st, r0):
        st["x"] = x_ref[r0:r0 + sub, :]
        h = _pre(st["x"], mod, gpre_ref[k:k + 1, :]).astype(BF16)
        st["z"] = _dot(h, win_ref[...])

    def pool(st, r0):
        a = st["z"][:, :pool_width]
        pos = (i * tm + r0 + 1 + lax.broadcasted_iota(jnp.int32, (sub, 1), 0)).astype(F32)
        level_sum = a
        y_a = []
        for g, w in enumerate(POOL_WINDOWS):
            level_sum = level_sum + shifted(level_sum, g, w // 2)
            mean = level_sum[:, :gd] / jnp.minimum(pos, float(w))
            dg = (mean - a[:, g * gd:(g + 1) * gd]).astype(BF16)
            y_a.append(_dot(dg, pool_w[g]) * pscale_ref[:, g * gd:(g + 1) * gd])
            level_sum = level_sum[:, gd:]
        st["y_a"] = jnp.concatenate(y_a, axis=-1).astype(BF16)

    def gate(st, r0):
        zb = jax.nn.gelu(st["z"][:, pool_width:])
        sgu_width = zb.shape[1] // 2
        hd = sgu_width // SGU_HEADS
        u = zb[:, :sgu_width]
        v = zb[:, sgu_width:]
        y_b = []
        for hh in range(SGU_HEADS):
            cols = slice(hh * hd, (hh + 1) * hd)
            vh = v[:, cols]
            mu = jnp.mean(vh, axis=-1, keepdims=True)
            var = jnp.mean(jnp.square(vh - mu), axis=-1, keepdims=True)
            vn = ((vh - mu) * lax.rsqrt(var + EPS) * lng_ref[:, cols] + lnb_ref[:, cols]).astype(BF16)
            s_h = [_dot(sgu_w[hh], vn[ck * SGU_CHUNK:(ck + 1) * SGU_CHUNK]) + sgub_ref[hh]
                   for ck in range(sub // SGU_CHUNK)]
            y_b.append(u[:, cols] * jnp.concatenate(s_h, axis=0))
        st["y_b"] = jnp.concatenate(y_b, axis=-1).astype(BF16)

    def finish(st, r0):
        y = _dot(st["y_a"], wout_ref[:pool_width, :]) + _dot(st["y_b"], wout_ref[pool_width:, :])
        o_ref[r0:r0 + sub, :] = _post(st["x"], y, mod, gpost_ref[k:k + 1, :], 1.0)

    _staggered((project, pool, gate, finish), n_split, sub)


def _mix0(x, mod, norm_pre, norm_post, sub, w_in, pool_w, pool_scale, ln_g, ln_b, sgu_w, sgu_b, w_out):
    bsz, s, d = x.shape
    pool_width = pool_scale.shape[0]
    sgu_width = ln_g.shape[0]
    hd = sgu_width // SGU_HEADS
    tm = min(ROW_TILE, s)
    sgu_b_rows = jnp.broadcast_to(sgu_b[:, :, None], (SGU_HEADS, SGU_CHUNK, hd))
    return pl.pallas_call(
        functools.partial(_mix0_kernel, k=sub[1], pool_width=pool_width, n_split=MIX_SPLIT),
        grid=(bsz, s // tm),
        in_specs=[_x_spec(tm, d), _mod_spec(sub, mod), _norm_spec(sub, d), _norm_spec(sub, d),
                  _const_spec(w_in.shape), _const_spec(pool_w.shape), _const_spec((1, pool_width)),
                  _const_spec((1, sgu_width)), _const_spec((1, sgu_width)),
                  _const_spec(sgu_w.shape), _const_spec(sgu_b_rows.shape), _const_spec(w_out.shape)],
        out_specs=_x_spec(tm, d),
        out_shape=jax.ShapeDtypeStruct(x.shape, F32),
        scratch_shapes=[pltpu.VMEM((len(POOL_WINDOWS), 8, pool_width), F32)],
        compiler_params=_params("mix0", "arbitrary", "arbitrary"),
    )(x, mod, norm_pre, norm_post, w_in, pool_w,
      pool_scale.reshape(1, pool_width), ln_g.reshape(1, sgu_width), ln_b.reshape(1, sgu_width),
      sgu_w, sgu_b_rows, w_out)


def _s5_in_kernel(x_ref, mod_ref, gpre_ref, w_ref, o_ref, u_ref, *, k, n_split):
    n_slab = o_ref.shape[0]
    sub = x_ref.shape[0] // n_split
    rows = sub // SSM_CHUNK
    mod = _mod_rows(mod_ref)

    def project(st, r0):
        h = _pre(x_ref[r0:r0 + sub, :], mod, gpre_ref[k:k + 1, :]).astype(BF16)
        st["u"] = _dot(h, w_ref[...])

    def relayout(st, r0):
        for j in range(n_slab):
            u_ref[j, r0:r0 + sub, :] = st["u"][:, j * LANES:(j + 1) * LANES]
        c0 = r0 // SSM_CHUNK
        for l in range(SSM_CHUNK):
            for j in range(n_slab):
                o_ref[j, c0:c0 + rows, l * LANES:(l + 1) * LANES] = (
                    u_ref[j, pl.ds(r0 + l, rows, stride=SSM_CHUNK), :].astype(BF16))

    _staggered((project, relayout), n_split, sub)


def _s5_in(x, mod, norm_pre, sub, w_in):
    bsz, s, d = x.shape
    width = w_in.shape[1]
    n_slab = width // LANES
    tm = min(S5_ROW_TILE, s)
    rows = tm // SSM_CHUNK
    tiles = s // tm
    return pl.pallas_call(
        functools.partial(_s5_in_kernel, k=sub[1], n_split=S5_SPLIT),
        grid=(bsz, tiles),
        in_specs=[_x_spec(tm, d), _mod_spec(sub, mod), _norm_spec(sub, d), _const_spec(w_in.shape)],
        out_specs=pl.BlockSpec((n_slab, rows, SSM_CHUNK * LANES), lambda b, i: (0, b * tiles + i, 0)),
        out_shape=jax.ShapeDtypeStruct((n_slab, bsz * s // SSM_CHUNK, SSM_CHUNK * LANES), BF16),
        scratch_shapes=[pltpu.VMEM((n_slab, tm, LANES), F32)],
        compiler_params=_params("s5_in", "arbitrary", "arbitrary"),
    )(x, mod, norm_pre, w_in)


def _s5_core_kernel(u_ref, lam_ref, bt_ref, ct_ref, d_ref, o_ref, st_ref, p_ref, plo_ref, qt_ref, toep_ref,
                    *, bsz, n_chunk):
    n_lt = st_ref.shape[0]
    half = n_lt // 2
    n_st = half * LANES
    u = u_ref[...]

    lam_re, lam_im = lam_ref[0:1], lam_ref[1:2]
    dt = jnp.exp(lam_ref[2:3])
    tau = lax.broadcasted_iota(jnp.int32, (SSM_CHUNK + 8, 1), 0).astype(F32)
    mag = jnp.exp(lam_re * dt * tau)
    ang = lam_im * dt * tau
    ap_re, ap_im = mag * jnp.cos(ang), mag * jnp.sin(ang)
    num_re, num_im = ap_re[1:2] - 1.0, ap_im[1:2]
    den = lam_re * lam_re + lam_im * lam_im
    cf_re = (num_re * lam_re + num_im * lam_im) / den
    cf_im = (num_im * lam_re - num_re * lam_im) / den

    n_state = n_st // SLAB_GROUPS
    same_group = (lax.broadcasted_iota(jnp.int32, (LANES, n_st), 0) // SSM_GROUP
                  == lax.broadcasted_iota(jnp.int32, (LANES, n_st), 1) // n_state)

    def block_diag(w):
        return jnp.where(same_group, jnp.concatenate([w] * (n_st // w.shape[1]), axis=-1), 0.0)

    bt_re, bt_im = block_diag(bt_ref[0]), block_diag(bt_ref[1])
    bb_re, bb_im = cf_re * bt_re - cf_im * bt_im, cf_re * bt_im + cf_im * bt_re
    ct_re, ct_im = block_diag(ct_ref[0]), block_diag(ct_ref[1])

    for l in range(SSM_CHUNK):
        blk = slice(l * LANES, (l + 1) * LANES)
        a_r, a_i = ap_re[SSM_CHUNK - 1 - l:SSM_CHUNK - l], ap_im[SSM_CHUNK - 1 - l:SSM_CHUNK - l]
        for part, cols in ((a_r * bb_re - a_i * bb_im, slice(0, n_st)),
                           (a_r * bb_im + a_i * bb_re, slice(n_st, 2 * n_st))):
            hi = part.astype(BF16)
            p_ref[blk, cols] = hi
            plo_ref[blk, cols] = (part - hi.astype(F32)).astype(BF16)

    c_nt = jnp.concatenate([ct_re, -ct_im], axis=-1)
    c_hi = c_nt.astype(BF16)
    c_lo = (c_nt - c_hi.astype(F32)).astype(BF16)
    imp_hi = lax.dot_general(p_ref[...], jnp.concatenate([c_hi, c_lo], axis=0), _NT,
                             preferred_element_type=F32)
    imp = (imp_hi[:, :LANES] + imp_hi[:, LANES:]
           + lax.dot_general(plo_ref[...], c_hi, _NT, preferred_element_type=F32))
    lag0 = (SSM_CHUNK - 1) * LANES
    toep_ref[:, LANES:] = imp.astype(BF16)
    toep_ref[:lag0, :LANES] = imp[LANES:].astype(BF16)
    toep_ref[lag0:, :LANES] = jnp.zeros((LANES, LANES), BF16)

    seq_rows = st_ref.shape[1] // bsz
    z = _dot(u, p_ref[...])
    for k in range(n_lt):
        for b in range(bsz):
            st_ref[k, b * seq_rows:b * seq_rows + n_chunk, :] = (
                z[b * n_chunk:(b + 1) * n_chunk, k * LANES:(k + 1) * LANES])

    n_tile = u.shape[1] // MXU_DIM
    col_tiles = [slice(n * MXU_DIM, (n + 1) * MXU_DIM) for n in range(n_tile)]
    d_row = jnp.concatenate([d_ref[...]] * (MXU_DIM // LANES), axis=-1)
    y_local = [_dot(u[:, :(n + 1) * MXU_DIM], toep_ref[(n_tile - 1 - n) * MXU_DIM:, :])
               + d_row * u[:, cols].astype(F32) for n, cols in enumerate(col_tiles)]

    for l in range(SSM_CHUNK):
        blk = slice(l * LANES, (l + 1) * LANES)
        a_r, a_i = ap_re[l + 1:l + 2], ap_im[l + 1:l + 2]
        qt_ref[blk, :n_st] = (a_r * ct_re - a_i * ct_im).astype(BF16)
        qt_ref[blk, n_st:] = (-(a_r * ct_im + a_i * ct_re)).astype(BF16)

    a_re = [ap_re[SSM_CHUNK:SSM_CHUNK + 1, k * LANES:(k + 1) * LANES] for k in range(half)]
    a_im = [ap_im[SSM_CHUNK:SSM_CHUNK + 1, k * LANES:(k + 1) * LANES] for k in range(half)]
    s_re = [jnp.zeros((bsz, LANES), F32)] * half
    s_im = [jnp.zeros((bsz, LANES), F32)] * half
    for c in range(n_chunk):
        rows = pl.ds(c, bsz, stride=seq_rows)
        for k in range(half):
            z_re = st_ref[k, rows, :]
            z_im = st_ref[half + k, rows, :]
            st_ref[k, rows, :] = s_re[k]
            st_ref[half + k, rows, :] = s_im[k]
            s_re[k], s_im[k] = (a_re[k] * s_re[k] - a_im[k] * s_im[k] + z_re,
                                a_re[k] * s_im[k] + a_im[k] * s_re[k] + z_im)
    st = jnp.concatenate(
        [jnp.concatenate([st_ref[k, b * seq_rows:b * seq_rows + n_chunk, :] for b in range(bsz)], axis=0)
         .astype(BF16) for k in range(n_lt)], axis=-1)

    for n, cols in enumerate(col_tiles):
        y = y_local[n] + lax.dot_general(st, qt_ref[cols, :], _NT, preferred_element_type=F32)
        o_ref[:, cols] = jax.nn.gelu(y).astype(BF16)


def _s5_core(u_cm, lam_rows, bt, ct, d_cm, bsz):
    n_slab, rows, kdim = u_cm.shape
    n_st = lam_rows.shape[2]
    slab = lambda shape: pl.BlockSpec((None,) + shape, lambda j: (j,) + (0,) * len(shape))
    return pl.pallas_call(
        functools.partial(_s5_core_kernel, bsz=bsz, n_chunk=rows // bsz),
        grid=(n_slab,),
        in_specs=[slab((rows, kdim)), slab(lam_rows.shape[1:]), slab(bt.shape[1:]), slab(ct.shape[1:]),
                  slab(d_cm.shape[1:])],
        out_specs=slab((rows, kdim)),
        out_shape=jax.ShapeDtypeStruct(u_cm.shape, BF16),
        scratch_shapes=[pltpu.VMEM((2 * n_st // LANES, rows + SUBLANES * bsz, LANES), F32),
                        pltpu.VMEM((kdim, 2 * n_st), BF16),
                        pltpu.VMEM((kdim, 2 * n_st), BF16),
                        pltpu.VMEM((kdim, 2 * n_st), BF16),
                        pltpu.VMEM((kdim, 2 * LANES), BF16)],
        compiler_params=_params("s5_core", "arbitrary"),
    )(u_cm, lam_rows, bt, ct, d_cm)


def _s5_out_kernel(x_ref, mod_ref, gpost_ref, g_ref, w_ref, o_ref, g_nat_ref, *, k, n_split):
    n_slab = g_ref.shape[0]
    sub = x_ref.shape[0] // n_split
    rows = sub // SSM_CHUNK
    d = w_ref.shape[1] // 2
    mod = _mod_rows(mod_ref)

    def relayout(st, r0):
        c0 = r0 // SSM_CHUNK
        for l in range(SSM_CHUNK):
            for j in range(n_slab):
                g_nat_ref[j, pl.ds(r0 + l, rows, stride=SSM_CHUNK), :] = (
                    g_ref[j, c0:c0 + rows, l * LANES:(l + 1) * LANES].astype(F32))

    def glu(st, r0):
        g = jnp.concatenate([g_nat_ref[j, r0:r0 + sub, :].astype(BF16) for j in range(n_slab)], axis=-1)
        st["ab"] = _dot(g, w_ref[...])

    def finish(st, r0):
        y = st["ab"][:, :d] * _sigmoid(st["ab"][:, d:])
        o_ref[r0:r0 + sub, :] = _post(x_ref[r0:r0 + sub, :], y, mod, gpost_ref[k:k + 1, :], 1.0)

    _staggered((relayout, glu, finish), n_split, sub)


def _s5_out(x, mod, norm_post, sub, g_cm, w_glu):
    bsz, s, d = x.shape
    n_slab = g_cm.shape[0]
    tm = min(S5_ROW_TILE, s)
    rows = tm // SSM_CHUNK
    tiles = s // tm
    return pl.pallas_call(
        functools.partial(_s5_out_kernel, k=sub[1], n_split=S5_SPLIT),
        grid=(bsz, tiles),
        in_specs=[_x_spec(tm, d), _mod_spec(sub, mod), _norm_spec(sub, d),
                  pl.BlockSpec((n_slab, rows, SSM_CHUNK * LANES), lambda b, i: (0, b * tiles + i, 0)),
                  _const_spec(w_glu.shape)],
        out_specs=_x_spec(tm, d),
        out_shape=jax.ShapeDtypeStruct(x.shape, F32),
        scratch_shapes=[pltpu.VMEM((n_slab, tm, LANES), F32)],
        compiler_params=_params("s5_out", "arbitrary", "arbitrary"),
    )(x, mod, norm_post, g_cm, w_glu)


def _s5_slab_params(lam_re, lam_im, b_re, b_im, c_re, c_im, d_skip, log_dt):
    n_group, n_state = lam_re.shape
    n_slab = n_group // SLAB_GROUPS
    n_st = SLAB_GROUPS * n_state
    log_dt_st = jnp.broadcast_to(log_dt[:, None], (n_group, n_state))
    lam_rows = jnp.stack([v.reshape(n_slab, n_st) for v in (lam_re, lam_im, log_dt_st)], axis=1)

    def rows_by_channel(w_re, w_im):
        w = jnp.stack([w_re, w_im], axis=0).reshape(2, n_slab, LANES, n_state).transpose(1, 0, 2, 3)
        return jnp.concatenate([w] * (LANES // n_state), axis=-1)

    bt = rows_by_channel(b_re.transpose(0, 2, 1), b_im.transpose(0, 2, 1))
    ct = rows_by_channel(c_re, c_im)
    return lam_rows, bt, ct, d_skip.reshape(n_slab, 1, LANES)


def kernel(x, c, ada_w, ada_b, norm_pre, norm_post, ffn_w_in, ffn_w_out, ab_w_in, pool_w, pool_scale, sgu_ln_g, sgu_ln_b, sgu_w, sgu_b, ab_w_out, ssm_w_in, ssm_lam_re, ssm_lam_im, ssm_b_re, ssm_b_im, ssm_c_re, ssm_c_im, ssm_d, ssm_log_dt, ssm_w_glu):
    bsz, s, d = x.shape
    depth = ada_w.shape[0]
    assert all(s % min(tile, s) == 0 for tile in (ROW_TILE, S5_ROW_TILE, FFN_ROW_TILE))

    def ffn_casts(l, half):
        return [(ffn_w_in, (l, half)), (ffn_w_out, (l, half))]

    def mixer_casts(l):
        i = l // 2
        return [(ab_w_in, (i,)), (ab_w_out, (i,))] if l % 2 == 0 else [(ssm_w_in, (i,)), (ssm_w_glu, (i,))]

    mod, w_ffn = _ada_mod(c, ada_w, ada_b, casts=ffn_casts(0, 0))
    for l in range(depth):
        i = l // 2
        x, cast = _ffn(x, mod, norm_pre, norm_post, (l, 0), *w_ffn, casts=mixer_casts(l) + ffn_casts(l, 1))
        w_mix, w_ffn = cast[:2], cast[2:]
        if l % 2 == 0:
            x = _mix0(x, mod, norm_pre, norm_post, (l, 1), w_mix[0], pool_w[i], pool_scale[i],
                      sgu_ln_g[i], sgu_ln_b[i], sgu_w[i], sgu_b[i], w_mix[1])
        else:
            slab_params = _s5_slab_params(ssm_lam_re[i], ssm_lam_im[i], ssm_b_re[i], ssm_b_im[i],
                                          ssm_c_re[i], ssm_c_im[i], ssm_d[i], ssm_log_dt[i])
            u_cm = _s5_in(x, mod, norm_pre, (l, 1), w_mix[0])
            g_cm = _s5_core(u_cm, *slab_params, bsz)
            x = _s5_out(x, mod, norm_post, (l, 1), g_cm, w_mix[1])
        x, w_ffn = _ffn(x, mod, norm_pre, norm_post, (l, 2), *w_ffn,
                        casts=ffn_casts(l + 1, 0) if l + 1 < depth else [])
    return x
```
